```python
import math
import jax
import jax.numpy as jnp
from jax import lax
import numpy as np

D_MODEL = 1024
BATCH = 1
SEQ = 16384
DEPTH = 2
DEC_BATCH = 128
DEC_SEQ = 1
PAST_LEN = 16384
PAGE_SIZE = 128

D_HEAD = 64
H_A = 8
H_IDX = 8
D_IDX = 64
TOPK_MAX = 256
H_B = 8
KV_B = 2
H_C = 8
D_NOPE = 64
D_ROPE = 32
D_V = 64
D_CQ = 256
D_LATENT = 128
ROPE_THETA = 10000.0
N_BUCKETS = 32
MAX_DISTANCE = 128
BRANCH_W = 512
N_BRANCH = 3
Q_BLOCK = 128
FORGET_BIAS_INIT = 3.0
LN_EPS = 1e-5
RMS_EPS = 1e-6
ALPHA = (2 * DEPTH) ** 0.25
BETA = (8 * DEPTH) ** -0.25

SPLITS = (
    ('a_q', H_A * D_HEAD), ('a_k', D_HEAD), ('a_v', D_HEAD),
    ('a_qidx', H_IDX * D_IDX), ('a_kidx', D_IDX), ('a_widx', H_IDX), ('a_z', BRANCH_W),
    ('b_q', H_B * D_HEAD), ('b_k', KV_B * D_HEAD), ('b_v', KV_B * D_HEAD), ('b_f', H_B), ('b_z', BRANCH_W),
    ('c_q', D_CQ), ('c_kv', D_LATENT), ('c_krope', D_ROPE), ('c_z', BRANCH_W),
    ('gates', N_BRANCH * D_MODEL),
)
D_IN = sum(n for _, n in SPLITS)

kernel_name = 'hybrid_dsa_fox_mla_gated_decoder_step'


def split_cols(u):
    out, off = {}, 0
    for name, n in SPLITS:
        out[name] = u[..., off:off + n]
        off += n
    return out


def take_rows(a, idx):
    return jax.vmap(lambda ab, ib: ab[ib])(a, idx)


def layer_norm(x, g, b):
    xf = x.astype(jnp.float32)
    mu = jnp.mean(xf, axis=-1, keepdims=True)
    var = jnp.mean(jnp.square(xf - mu), axis=-1, keepdims=True)
    return ((xf - mu) * lax.rsqrt(var + LN_EPS) * g + b).astype(x.dtype)


def rms_norm(x, g):
    xf = x.astype(jnp.float32)
    return (xf * lax.rsqrt(jnp.mean(xf * xf, axis=-1, keepdims=True) + RMS_EPS) * g).astype(x.dtype)


def rope(x, pos):
    half = D_ROPE // 2
    freqs = ROPE_THETA ** (-jnp.arange(half, dtype=jnp.float32) / half)
    ang = pos.astype(jnp.float32)[:, None] * freqs
    ang = ang.reshape(ang.shape[:1] + (1,) * (x.ndim - 3) + (half,))
    cos, sin = jnp.cos(ang).astype(x.dtype), jnp.sin(ang).astype(x.dtype)
    x1, x2 = x[..., :half], x[..., half:]
    return jnp.concatenate([x1 * cos - x2 * sin, x1 * sin + x2 * cos], axis=-1)


def t5_bucket(dist):
    max_exact = N_BUCKETS // 2
    d = jnp.maximum(dist, 1).astype(jnp.float32)
    large = max_exact + (jnp.log(d / max_exact) / math.log(MAX_DISTANCE / max_exact)
                         * (N_BUCKETS - max_exact)).astype(jnp.int32)
    large = jnp.minimum(large, N_BUCKETS - 1)
    return jnp.where(dist < max_exact, dist, large)


def indexer_scores(q_idx, w_idx, k_idx):
    r = jax.nn.relu(jnp.einsum('bthd,bsd->bths', q_idx, k_idx))
    return jnp.einsum('bths,bth->bts', r, w_idx).astype(jnp.float32)


def dsa_attend(q, q_idx, w_idx, qpos, kidx_all, gather_kv, t5_table, topk):
    b, t = q.shape[:2]
    n_keys = kidx_all.shape[1]
    causal = jnp.arange(n_keys)[None, :] <= qpos[:, None]
    sc = jnp.where(causal, indexer_scores(q_idx, w_idx, kidx_all), -jnp.inf)
    _, idx = lax.top_k(sc, topk)
    ks, vs = gather_kv(idx)
    dist = qpos[None, :, None] - idx
    valid = dist >= 0
    s = jnp.einsum('bthd,btkd->bthk', q, ks).astype(jnp.float32) * (D_HEAD ** -0.5)
    bias = t5_table[t5_bucket(jnp.maximum(dist, 0))].astype(jnp.float32)
    s = jnp.where(valid[:, :, None, :], s + jnp.swapaxes(bias, -1, -2), -jnp.inf)
    p = jax.nn.softmax(s, axis=-1).astype(vs.dtype)
    return jnp.einsum('bthk,btkd->bthd', p, vs).reshape(b, t, H_A * D_HEAD)


def fox_attend(q, k, v, fq, fk, qpos):
    b, t = q.shape[:2]
    n_keys = k.shape[1]
    g = H_B // KV_B
    qg = q.reshape(b, t, KV_B, g, D_HEAD)
    s = jnp.einsum('btkgd,bskd->bkgts', qg, k).astype(jnp.float32) * (D_HEAD ** -0.5)
    fq_t = jnp.transpose(fq.reshape(b, t, KV_B, g), (0, 2, 3, 1))
    fk_t = jnp.transpose(fk.reshape(b, n_keys, KV_B, g), (0, 2, 3, 1))
    causal = jnp.arange(n_keys)[None, :] <= qpos[:, None]
    s = jnp.where(causal, s + (fq_t[..., :, None] - fk_t[..., None, :]), -jnp.inf)
    p = jax.nn.softmax(s, axis=-1).astype(v.dtype)
    return jnp.einsum('bkgts,bskd->btkgd', p, v).reshape(b, t, H_B * D_HEAD)


def mla_attend(q_nope, q_rope, ckv, krope, w_uk, w_uv, qpos):
    b, t = q_nope.shape[:2]
    n_keys = ckv.shape[1]
    q_lat = jnp.einsum('bthn,chn->bthc', q_nope, w_uk)
    s = jnp.einsum('bthc,bsc->bhts', q_lat, ckv) + jnp.einsum('bthr,bsr->bhts', q_rope, krope)
    s = s.astype(jnp.float32) * ((D_NOPE + D_ROPE) ** -0.5)
    causal = jnp.arange(n_keys)[None, :] <= qpos[:, None]
    s = jnp.where(causal, s, -jnp.inf)
    p = jax.nn.softmax(s, axis=-1).astype(ckv.dtype)
    o_lat = jnp.einsum('bhts,bsc->bthc', p, ckv)
    return jnp.einsum('bthc,chv->bthv', o_lat, w_uv).reshape(b, t, H_C * D_V)


def block_sweep(fn, q_args):
    n = q_args[0].shape[1]

    def body(i):
        t0 = i * Q_BLOCK
        blk = [lax.dynamic_slice_in_dim(a, t0, Q_BLOCK, axis=1) for a in q_args]
        return fn(*blk, t0 + jnp.arange(Q_BLOCK, dtype=jnp.int32))

    out = lax.map(body, jnp.arange(n // Q_BLOCK))
    nb, b, qb, w = out.shape
    return jnp.swapaxes(out, 0, 1).reshape(b, nb * qb, w)


def gather_pages(cache, l, page_table):
    g = cache[l, page_table]
    return g.reshape((g.shape[0], g.shape[1] * g.shape[2]) + g.shape[3:])


def paged_gather(cache, l, page_table, idx, new_rows):
    pi = jnp.minimum(idx, PAST_LEN - 1)
    phys = take_rows(page_table, pi // PAGE_SIZE)
    past = cache[l, phys, pi % PAGE_SIZE]
    new = take_rows(new_rows, jnp.clip(idx - PAST_LEN, 0, new_rows.shape[1] - 1))
    return jnp.where((idx < PAST_LEN)[..., None], past, new)


def layer_inputs(x, pos, w_in_l, b_forget_l, w_c_uq_l, g_c_q_l, g_c_kv_l):
    b, t = x.shape[:2]
    p = split_cols(jnp.einsum('btd,de->bte', x, w_in_l))
    quq = jnp.einsum('btc,chd->bthd', rms_norm(p['c_q'], g_c_q_l), w_c_uq_l)
    return dict(
        a_q=p['a_q'].reshape(b, t, H_A, D_HEAD), a_k=p['a_k'], a_v=p['a_v'],
        a_qidx=p['a_qidx'].reshape(b, t, H_IDX, D_IDX), a_kidx=p['a_kidx'], a_widx=p['a_widx'],
        b_q=p['b_q'].reshape(b, t, H_B, D_HEAD),
        b_k=p['b_k'].reshape(b, t, KV_B, D_HEAD), b_v=p['b_v'].reshape(b, t, KV_B, D_HEAD),
        b_logf=jax.nn.log_sigmoid(p['b_f'].astype(jnp.float32) + b_forget_l.astype(jnp.float32)),
        c_qnope=quq[..., :D_NOPE], c_qrope=rope(quq[..., D_NOPE:], pos),
        c_kv=rms_norm(p['c_kv'], g_c_kv_l), c_krope=rope(p['c_krope'], pos),
        z=(p['a_z'], p['b_z'], p['c_z']), gates=p['gates'])


def state_rows(p):
    return (p['a_k'], p['a_v'], p['a_kidx'], p['b_k'], p['b_v'], p['b_logf'], p['c_kv'], p['c_krope'])


def merge_and_norm(x, o_a, o_b, o_c, p, w_branch_l, w_out_l, ln_g_l, ln_b_l):
    b, t = x.shape[:2]
    z_a, z_b, z_c = p['z']
    o = jnp.stack([o_a * jax.nn.silu(z_a), o_b * jax.nn.silu(z_b), o_c * jax.nn.silu(z_c)], axis=2)
    y = jnp.einsum('btnw,nwd->btnd', o, w_branch_l)
    gate = jax.nn.sigmoid(p['gates'].reshape(b, t, N_BRANCH, D_MODEL))
    mixed = jnp.einsum('btnd,btnd->btd', gate, y)
    out = jnp.einsum('btd,de->bte', mixed, w_out_l)
    return layer_norm(ALPHA * x + out, ln_g_l, ln_b_l)


def setup_inputs(seed: int = 0) -> dict:
    key = jax.random.key(seed)
    k = jax.random.split(key, 24)
    nrm = jax.random.normal
    n_pages = PAST_LEN // PAGE_SIZE
    n_used = DEC_BATCH * n_pages
    n_pool = n_used + max(1, n_used // 4)

    def cache(kk, *tail):
        return nrm(kk, (DEPTH, n_pool, PAGE_SIZE) + tail, jnp.float32)

    page_table = jax.random.permutation(k[10], n_pool)[:n_used].reshape(DEC_BATCH, n_pages).astype(jnp.int32)
    return {
        'x_prompt': nrm(k[0], (BATCH, SEQ, D_MODEL), jnp.float32),
        'x_sample': nrm(k[1], (DEC_BATCH, DEC_SEQ, D_MODEL), jnp.float32),
        'cache_a_k': cache(k[2], D_HEAD),
        'cache_a_v': cache(k[3], D_HEAD),
        'cache_a_kidx': cache(k[4], D_IDX),
        'cache_b_k': cache(k[5], KV_B, D_HEAD),
        'cache_b_v': cache(k[6], KV_B, D_HEAD),
        'cache_b_logf': jax.nn.log_sigmoid(FORGET_BIAS_INIT + cache(k[7], H_B)),
        'cache_c_latent': cache(k[8], D_LATENT),
        'cache_c_krope': cache(k[9], D_ROPE),
        'page_table': page_table,
        't5_table': 0.5 * nrm(k[11], (N_BUCKETS, H_A), jnp.float32),
        'w_in': nrm(k[12], (DEPTH, D_MODEL, D_IN), jnp.float32) * D_MODEL ** -0.5,
        'b_forget': FORGET_BIAS_INIT + 0.1 * nrm(k[13], (DEPTH, H_B), jnp.float32),
        'w_c_uq': nrm(k[14], (DEPTH, D_CQ, H_C, D_NOPE + D_ROPE), jnp.float32) * D_CQ ** -0.5,
        'g_c_q': 1.0 + 0.05 * nrm(k[15], (DEPTH, D_CQ), jnp.float32),
        'g_c_kv': 1.0 + 0.05 * nrm(k[16], (DEPTH, D_LATENT), jnp.float32),
        'w_c_ukv': nrm(k[17], (DEPTH, D_LATENT, H_C, D_NOPE + D_V), jnp.float32) * D_LATENT ** -0.5,
        'w_branch': nrm(k[18], (DEPTH, N_BRANCH, BRANCH_W, D_MODEL), jnp.float32) * (BRANCH_W ** -0.5 * BETA),
        'w_out': nrm(k[19], (DEPTH, D_MODEL, D_MODEL), jnp.float32) * (D_MODEL ** -0.5 * BETA),
        'ln_g': 1.0 + 0.05 * nrm(k[20], (DEPTH, D_MODEL), jnp.float32),
        'ln_b': 0.02 * nrm(k[21], (DEPTH, D_MODEL), jnp.float32),
    }


def reference(x_prompt, x_sample, cache_a_k, cache_a_v, cache_a_kidx, cache_b_k, cache_b_v, cache_b_logf,
              cache_c_latent, cache_c_krope, page_table, t5_table, w_in, b_forget, w_c_uq, g_c_q, g_c_kv,
              w_c_ukv, w_branch, w_out, ln_g, ln_b):
    pos_p = jnp.arange(SEQ, dtype=jnp.int32)
    pos_s = PAST_LEN + jnp.arange(DEC_SEQ, dtype=jnp.int32)
    topk_p = min(TOPK_MAX, SEQ // 4)
    topk_s = min(TOPK_MAX, (PAST_LEN + DEC_SEQ) // 4)
    xp, xs = x_prompt, x_sample
    rows_p, rows_s = [], []
    for l in range(DEPTH):
        w_l = (w_in[l], b_forget[l], w_c_uq[l], g_c_q[l], g_c_kv[l])
        w_uk, w_uv = w_c_ukv[l][..., :D_NOPE], w_c_ukv[l][..., D_NOPE:]

        p = layer_inputs(xp, pos_p, *w_l)
        o_a = block_sweep(
            lambda q, qi, wi, qpos: dsa_attend(
                q, qi, wi, qpos, p['a_kidx'],
                lambda idx: (take_rows(p['a_k'], idx), take_rows(p['a_v'], idx)), t5_table, topk_p),
            [p['a_q'], p['a_qidx'], p['a_widx']])
        f_p = jnp.cumsum(p['b_logf'], axis=1)
        o_b = block_sweep(lambda q, fq, qpos: fox_attend(q, p['b_k'], p['b_v'], fq, f_p, qpos), [p['b_q'], f_p])
        o_c = block_sweep(
            lambda qn, qr, qpos: mla_attend(qn, qr, p['c_kv'], p['c_krope'], w_uk, w_uv, qpos),
            [p['c_qnope'], p['c_qrope']])
        rows_p.append(state_rows(p))
        xp = merge_and_norm(xp, o_a, o_b, o_c, p, w_branch[l], w_out[l], ln_g[l], ln_b[l])

        s = layer_inputs(xs, pos_s, *w_l)
        kidx_all = jnp.concatenate([gather_pages(cache_a_kidx, l, page_table), s['a_kidx']], axis=1)
        o_a = dsa_attend(
            s['a_q'], s['a_qidx'], s['a_widx'], pos_s, kidx_all,
            lambda idx: (paged_gather(cache_a_k, l, page_table, idx, s['a_k']),
                         paged_gather(cache_a_v, l, page_table, idx, s['a_v'])),
            t5_table, topk_s)
        k_b = jnp.concatenate([gather_pages(cache_b_k, l, page_table), s['b_k']], axis=1)
        v_b = jnp.concatenate([gather_pages(cache_b_v, l, page_table), s['b_v']], axis=1)
        f_s = jnp.cumsum(jnp.concatenate(
            [gather_pages(cache_b_logf, l, page_table).astype(jnp.float32), s['b_logf']], axis=1), axis=1)
        o_b = fox_attend(s['b_q'], k_b, v_b, f_s[:, PAST_LEN:], f_s, pos_s)
        ckv = jnp.concatenate([gather_pages(cache_c_latent, l, page_table), s['c_kv']], axis=1)
        krope = jnp.concatenate([gather_pages(cache_c_krope, l, page_table), s['c_krope']], axis=1)
        o_c = mla_attend(s['c_qnope'], s['c_qrope'], ckv, krope, w_uk, w_uv, pos_s)
        rows_s.append(state_rows(s))
        xs = merge_and_norm(xs, o_a, o_b, o_c, s, w_branch[l], w_out[l], ln_g[l], ln_b[l])

    (new_a_k_p, new_a_v_p, new_a_kidx_p, new_b_k_p, new_b_v_p, new_b_logf_p, new_c_latent_p,
     new_c_krope_p) = [jnp.stack([r[i] for r in rows_p], axis=0) for i in range(8)]
    (new_a_k_s, new_a_v_s, new_a_kidx_s, new_b_k_s, new_b_v_s, new_b_logf_s, new_c_latent_s,
     new_c_krope_s) = [jnp.stack([r[i] for r in rows_s], axis=0) for i in range(8)]
    y_prompt, y_sample = xp, xs
    return (y_prompt, y_sample,
            new_a_k_p, new_a_v_p, new_a_kidx_p, new_b_k_p, new_b_v_p, new_b_logf_p, new_c_latent_p, new_c_krope_p,
            new_a_k_s, new_a_v_s, new_a_kidx_s, new_b_k_s, new_b_v_s, new_b_logf_s, new_c_latent_s, new_c_krope_s)
```

```python
import functools

import numpy as np
import jax
import jax.numpy as jnp
from jax import lax
from jax.experimental import pallas as pl
from jax.experimental.pallas import tpu as pltpu

F32 = jnp.float32
BF16 = jnp.bfloat16
I32 = jnp.int32

N_HEADS = 8
D_HEAD = 64
KV_B = 2
D_NOPE = 64
D_ROPE = 32
D_CQ = 256
D_LATENT = 128
TOPK_MAX = 256
ROPE_THETA = 10000.0
N_BUCKETS = 32
MAX_DISTANCE = 128
BRANCH_W = 512
D_MODEL = 1024
PAGE = 128
LN_EPS = 1e-5
RMS_EPS = 1e-6

LANES = 128
NEG = -1e30
INT_MIN = int(np.iinfo(np.int32).min)
VMEM_LIMIT = 56 * 1024 * 1024

SRC_SPLITS = (
    ('a_q', 512), ('a_k', 64), ('a_v', 64), ('a_qidx', 512), ('a_kidx', 64), ('a_widx', 8), ('a_z', 512),
    ('b_q', 512), ('b_k', 128), ('b_v', 128), ('b_f', 8), ('b_z', 512),
    ('c_q', 256), ('c_kv', 128), ('c_krope', 32), ('c_z', 512), ('gates', 3072))
NP = 7168
OFF_Q = 0
OFF_Z = 1536
OFF_G = 3072
OFF_S = 6144
L_KROPE, L_BF, L_WIDX = 64, 96, 104


def _fox_head_perm():
    j = np.arange(BRANCH_W)
    return ((j // 128) + 4 * ((j % 128) // 64)) * 64 + (j % 64)


def _col_perm():
    src_off, o = {}, 0
    for name, n in SRC_SPLITS:
        src_off[name] = o
        o += n
    src = -np.ones((NP,), np.int64)
    scale = np.ones((NP,), np.float32)

    def put(dst, name, n, perm=None):
        idx = np.arange(n) if perm is None else perm
        src[dst:dst + n] = src_off[name] + idx

    put(0, 'a_q', 512)
    scale[0:512] = D_HEAD ** -0.5
    put(512, 'a_qidx', 512)
    put(1024, 'b_q', 512)
    scale[1024:1536] = D_HEAD ** -0.5
    put(OFF_Z, 'a_z', 512)
    put(OFF_Z + 512, 'b_z', 512, _fox_head_perm())
    put(OFF_Z + 1024, 'c_z', 512)
    put(OFF_G, 'gates', 3072)
    put(OFF_S, 'c_q', 256)
    g0 = OFF_S + 256
    put(g0, 'a_k', 64)
    put(g0 + 64, 'a_v', 64)
    g1 = g0 + 128
    put(g1, 'a_kidx', 64)
    put(g1 + L_KROPE, 'c_krope', 32)
    put(g1 + L_BF, 'b_f', 8)
    put(g1 + L_WIDX, 'a_widx', 8)
    put(g1 + 128, 'b_k', 128)
    put(g1 + 256, 'b_v', 128)
    put(g1 + 384, 'c_kv', 128)
    scale[src < 0] = 0.0
    return np.maximum(src, 0), scale


def _split3(x):
    hi = x.astype(BF16)
    r = x - hi.astype(F32)
    mid = r.astype(BF16)
    lo = (r - mid.astype(F32)).astype(BF16)
    return lo, mid, hi


def _nt(a, b):
    return lax.dot_general(a, b, (((1,), (1,)), ((), ())), preferred_element_type=F32)


def _cparams(sem):
    return pltpu.CompilerParams(dimension_semantics=sem, vmem_limit_bytes=VMEM_LIMIT)


def _mm_kernel(x_ref, w_ref, o_ref, xb_ref):
    @pl.when(pl.program_id(1) == 0)
    def _():
        xb_ref[...] = x_ref[...].astype(BF16)

    o_ref[...] = jnp.dot(xb_ref[...], w_ref[...], preferred_element_type=F32)


def _project(x, wp):
    m, k = x.shape
    n = wp.shape[1]
    tm = min(m, 1024)
    tn = 1024
    return pl.pallas_call(
        _mm_kernel,
        grid=(m // tm, n // tn),
        in_specs=[pl.BlockSpec((tm, k), lambda i, j: (i, 0)),
                  pl.BlockSpec((k, tn), lambda i, j: (0, j))],
        out_specs=pl.BlockSpec((tm, tn), lambda i, j: (i, j)),
        out_shape=jax.ShapeDtypeStruct((m, n), F32),
        scratch_shapes=[pltpu.VMEM((tm, k), BF16)],
        compiler_params=_cparams(("arbitrary", "arbitrary")),
        name="proj_mm",
    )(x, wp)


def _post_kernel(yq_ref, ys_ref, cosq_ref, sinq_ref, cosk_ref, sink_ref, bf_ref, gq_ref, gkv_ref,
                 sela_ref, selb_ref, wq2_ref, wuk_ref, psel_ref,
                 qa_ref, qi_ref, qb_ref, qc_ref, st_ref, kvb_ref, *, scale_c):
    yq = yq_ref[...]

    def heads(x, sel_ref, out_ref):
        full = jnp.dot(x.astype(BF16), sel_ref[...], preferred_element_type=F32)
        for h in range(N_HEADS):
            out_ref[h] = full[:, h * LANES:(h + 1) * LANES].astype(BF16)

    heads(yq[:, 0:512], sela_ref, qa_ref)
    heads(yq[:, 512:1024], sela_ref, qi_ref)
    heads(yq[:, 1024:1536], selb_ref, qb_ref)

    ys = ys_ref[...]
    cq = ys[:, 0:D_CQ]
    cqn = cq * lax.rsqrt(jnp.mean(cq * cq, axis=-1, keepdims=True) + RMS_EPS) * gq_ref[...]
    quq = jnp.dot(cqn.astype(BF16), wq2_ref[...], preferred_element_type=F32)
    x1, x2 = quq[:, 512:640], quq[:, 640:768]
    cos, sin = cosq_ref[...], sinq_ref[...]
    rq = jnp.concatenate([x1 * cos - x2 * sin, x1 * sin + x2 * cos], axis=1) * scale_c
    rsel = jnp.dot(rq.astype(BF16), psel_ref[...], preferred_element_type=F32)
    qlat = jnp.dot(quq[:, 0:512].astype(BF16), wuk_ref[...], preferred_element_type=F32) * scale_c
    for h in range(N_HEADS):
        qc_ref[h, :, 0:LANES] = qlat[:, h * LANES:(h + 1) * LANES].astype(BF16)
        qc_ref[h, :, LANES:2 * LANES] = rsel[:, h * LANES:(h + 1) * LANES].astype(BF16)

    g0 = ys[:, 256:384]
    g1 = ys[:, 384:512]
    g2 = ys[:, 512:640]
    g3 = ys[:, 640:768]
    ckv = ys[:, 768:896]
    ckvn = ckv * lax.rsqrt(jnp.mean(ckv * ckv, axis=-1, keepdims=True) + RMS_EPS) * gkv_ref[...]
    lane = lax.broadcasted_iota(I32, g1.shape, 1)
    swap = jnp.where(lane < L_KROPE + 16, pltpu.roll(g1, LANES - 16, 1), pltpu.roll(g1, 16, 1))
    roped = g1 * cosk_ref[...] + swap * sink_ref[...]
    v = g1 + bf_ref[...]
    logsig = jnp.minimum(v, 0.0) - jnp.log1p(jnp.exp(-jnp.abs(v)))
    g1p = jnp.where((lane >= L_BF) & (lane < L_BF + N_HEADS), logsig, roped)
    v2 = jnp.where(lane < D_HEAD, pltpu.roll(g0, D_HEAD, 1), g0)

    st_ref[:, 0:128] = g0
    st_ref[:, 128:256] = g1p
    st_ref[:, 256:384] = g2
    st_ref[:, 384:512] = g3
    st_ref[:, 512:640] = ckvn
    kvb_ref[:, 0:128] = g0.astype(BF16)
    kvb_ref[:, 128:256] = g1p.astype(BF16)
    kvb_ref[:, 256:384] = g2.astype(BF16)
    kvb_ref[:, 384:512] = g3.astype(BF16)
    kvb_ref[:, 512:640] = ckvn.astype(BF16)
    kvb_ref[:, 640:768] = pltpu.roll(g1p, LANES - L_KROPE, 1).astype(BF16)
    kvb_ref[:, 768:896] = v2.astype(BF16)
    kvb_ref[:, 896:1024] = jnp.zeros_like(g0).astype(BF16)


def _post(y, tabs, consts):
    m = y.shape[0]
    tm = min(m, 256)
    row = lambda w: pl.BlockSpec((tm, w), lambda i: (i, 0))
    full = lambda a: pl.BlockSpec(a.shape, lambda i: (0,) * a.ndim)
    cosq, sinq, cosk, sink = tabs
    bf_row, gq, gkv, sela, selb, wq2, wuk, psel = consts
    hq = lambda w: pl.BlockSpec((N_HEADS, tm, w), lambda i: (0, i, 0))
    kern = functools.partial(_post_kernel, scale_c=float((D_NOPE + D_ROPE) ** -0.5))
    return pl.pallas_call(
        kern,
        grid=(m // tm,),
        in_specs=[pl.BlockSpec((tm, 1536), lambda i: (i, 0)),
                  pl.BlockSpec((tm, 1024), lambda i: (i, OFF_S // 1024)),
                  row(128), row(128), row(128), row(128),
                  full(bf_row), full(gq), full(gkv), full(sela), full(selb), full(wq2), full(wuk), full(psel)],
        out_specs=[hq(128), hq(128), hq(128), hq(256), row(640), row(1024)],
        out_shape=[jax.ShapeDtypeStruct((N_HEADS, m, 128), BF16),
                   jax.ShapeDtypeStruct((N_HEADS, m, 128), BF16),
                   jax.ShapeDtypeStruct((N_HEADS, m, 128), BF16),
                   jax.ShapeDtypeStruct((N_HEADS, m, 256), BF16),
                   jax.ShapeDtypeStruct((m, 640), F32),
                   jax.ShapeDtypeStruct((m, 1024), BF16)],
        compiler_params=_cparams(("arbitrary",)),
        name="post_proj",
    )(y, y, cosq, sinq, cosk, sink, bf_row, gq, gkv, sela, selb, wq2, wuk, psel)


def _page_prefix_kernel(x_ref, m1_ref, m2_ref, o_ref):
    parts = _split3(x_ref[...])
    w = None
    t = None
    for p in parts:
        dw = jnp.dot(p, m1_ref[...], preferred_element_type=F32)
        dt = jnp.dot(p, m2_ref[...], preferred_element_type=F32)
        w = dw if w is None else w + dw
        t = dt if t is None else t + dt
    o_ref[:, 0:1024] = w
    o_ref[:, 1024:2048] = t


def _page_prefix(x, m1, m2):
    r = x.shape[0]
    tr = min(r, 512)
    assert r % tr == 0
    return pl.pallas_call(
        _page_prefix_kernel,
        grid=(r // tr,),
        in_specs=[pl.BlockSpec((tr, 1024), lambda i: (i, 0)),
                  pl.BlockSpec((1024, 1024), lambda i: (0, 0)),
                  pl.BlockSpec((1024, 1024), lambda i: (0, 0))],
        out_specs=pl.BlockSpec((tr, 2048), lambda i: (i, 0)),
        out_shape=jax.ShapeDtypeStruct((r, 2048), F32),
        compiler_params=_cparams(("arbitrary",)),
        name="page_prefix",
    )(x, m1, m2)


def _blk_prefix_kernel(wt_ref, ltri_ref, o_ref):
    w = wt_ref[:, 0:1024]
    acc = None
    for p in _split3(wt_ref[:, 1024:2048]):
        d = jnp.dot(ltri_ref[...], p, preferred_element_type=F32)
        acc = d if acc is None else acc + d
    o_ref[...] = -(w + acc)


def _blk_prefix(wt, ltri):
    nb = wt.shape[0]
    return pl.pallas_call(
        _blk_prefix_kernel,
        grid=(1,),
        in_specs=[pl.BlockSpec((nb, 2048), lambda i: (0, 0)), pl.BlockSpec((nb, nb), lambda i: (0, 0))],
        out_specs=pl.BlockSpec((nb, 1024), lambda i: (0, 0)),
        out_shape=jax.ShapeDtypeStruct((nb, 1024), F32),
        compiler_params=_cparams(("arbitrary",)),
        name="blk_prefix",
    )(wt, ltri)


def _softmax_update(s_list, v, m_ref, l_ref, acc_ref, tq):
    tk = s_list[0].shape[1]
    ps, alphas = [], []
    for h in range(N_HEADS):
        rows = slice(h * tq, (h + 1) * tq)
        s = s_list[h]
        m_prev = m_ref[rows, :]
        m_new = jnp.maximum(m_prev, jnp.max(s, axis=1, keepdims=True))
        alpha = jnp.exp(m_prev - m_new)
        p = jnp.exp(s - jnp.concatenate([m_new] * (tk // LANES), axis=1))
        l_ref[rows, :] = alpha * l_ref[rows, :] + jnp.sum(p, axis=1, keepdims=True)
        m_ref[rows, :] = m_new
        ps.append(p.astype(BF16))
        alphas.append(alpha)
    pv = jnp.dot(jnp.concatenate(ps, axis=0), v, preferred_element_type=F32)
    acc_ref[...] = jnp.concatenate(alphas, axis=0) * acc_ref[...] + pv


def _init_state(m_ref, l_ref, acc_ref):
    m_ref[...] = jnp.full(m_ref.shape, NEG, F32)
    l_ref[...] = jnp.zeros(l_ref.shape, F32)
    acc_ref[...] = jnp.zeros(acc_ref.shape, F32)


def _float_key(x):
    bits = pltpu.bitcast(x, I32)
    return bits ^ ((bits >> 31) & 0x7FFFFFFF)


def _kth_largest_key(count_ge, k, shape):
    zero = jnp.zeros(shape, I32)
    prefix = jnp.where(count_ge(zero) >= k, zero, jnp.full(shape, INT_MIN, I32))

    def bit_body(it, prefix):
        cand = prefix + jnp.left_shift(jnp.int32(1), 30 - it)
        return jnp.where(count_ge(cand) >= k, cand, prefix)

    return lax.fori_loop(0, 31, bit_body, prefix)


def _dsa_prompt_kernel(b31_ref, qi_ref, qa_ref, w_ref, k0_ref, k1_ref, v2_ref, btile_ref, o_ref,
                       key_scr, m_scr, l_scr, acc_scr, *, tq, tk, topk):
    i = pl.program_id(0)
    t0 = i * tq
    nsl = tk // LANES
    qi = qi_ref[...].reshape(N_HEADS * tq, LANES)
    qa = qa_ref[...].reshape(N_HEADS * tq, LANES)
    w = w_ref[...]
    wb = [jnp.broadcast_to(w[:, L_WIDX + h:L_WIDX + h + 1], (tq, tk)) for h in range(N_HEADS)]
    qpos = t0 + lax.broadcasted_iota(I32, (tq, tk), 0)
    lane_k = lax.broadcasted_iota(I32, (tq, tk), 1)
    nc = (t0 + tq + tk - 1) // tk

    def score_body(c, carry):
        k0 = pl.multiple_of(c * tk, tk)
        y = _nt(qi, k1_ref[pl.ds(k0, tk), :])
        acc = wb[0] * jnp.maximum(y[0:tq], 0.0)
        for h in range(1, N_HEADS):
            acc = acc + wb[h] * jnp.maximum(y[h * tq:(h + 1) * tq], 0.0)
        acc = jnp.where(k0 + lane_k <= qpos, acc, -jnp.inf)
        key = _float_key(acc)
        for j in range(nsl):
            key_scr[c * nsl + j] = key[:, j * LANES:(j + 1) * LANES]
        return carry

    lax.fori_loop(0, nc, score_body, 0)

    def count_ge(cand):
        def body(c, cnt):
            for j in range(nsl):
                cnt = cnt + (key_scr[c * nsl + j] >= cand).astype(I32)
            return cnt

        cnt = lax.fori_loop(0, nc, body, jnp.zeros((tq, LANES), I32))
        return jnp.sum(cnt.astype(F32), axis=1, keepdims=True)

    thr = _kth_largest_key(count_ge, float(topk), (tq, LANES))

    _init_state(m_scr, l_scr, acc_scr)
    far_end = jnp.maximum(t0 - tq, 0)
    nf = (far_end + tk - 1) // tk
    thr_k = jnp.concatenate([thr] * nsl, axis=1)

    def far_body(c, carry):
        k0 = pl.multiple_of(c * tk, tk)
        s = _nt(qa, k0_ref[pl.ds(k0, tk), :])
        keyc = jnp.concatenate([key_scr[c * nsl + j] for j in range(nsl)], axis=1)
        sel = (keyc >= thr_k) & (k0 + lane_k < far_end)
        s_list = [jnp.where(sel, s[h * tq:(h + 1) * tq] + b31_ref[h], NEG) for h in range(N_HEADS)]
        _softmax_update(s_list, v2_ref[pl.ds(k0, tk), :], m_scr, l_scr, acc_scr, tq)
        return carry

    lax.fori_loop(0, nf, far_body, 0)

    ns = pl.multiple_of(far_end, tq)
    sl = far_end // LANES
    s = _nt(qa, k0_ref[pl.ds(ns, 2 * tq), :])
    keyw = jnp.concatenate([key_scr[sl], key_scr[sl + 1]], axis=1)
    kposw = ns + lax.broadcasted_iota(I32, (tq, 2 * tq), 1)
    qposw = t0 + lax.broadcasted_iota(I32, (tq, 2 * tq), 0)
    selw = (keyw >= jnp.concatenate([thr, thr], axis=1)) & (kposw <= qposw)
    first = jnp.where(i == 0, 1, 0)
    s_list = []
    for h in range(N_HEADS):
        bias = jnp.concatenate([btile_ref[first, h], btile_ref[1, h]], axis=1)
        s_list.append(jnp.where(selw, s[h * tq:(h + 1) * tq] + bias, NEG))
    _softmax_update(s_list, v2_ref[pl.ds(ns, 2 * tq), :], m_scr, l_scr, acc_scr, tq)

    lane = lax.broadcasted_iota(I32, (tq, LANES), 1)
    for c in range(N_HEADS // 2):
        r0 = slice(2 * c * tq, (2 * c + 1) * tq)
        r1 = slice((2 * c + 1) * tq, (2 * c + 2) * tq)
        o_ref[:, c * LANES:(c + 1) * LANES] = jnp.where(
            lane < D_HEAD, acc_scr[r0, :] / l_scr[r0, :], acc_scr[r1, :] / l_scr[r1, :])


def _dsa_prompt(b31, qi, qa, st, kvb, btile, topk):
    s = st.shape[0]
    tq, tk = 128, min(512, s)
    assert s % tk == 0 and s >= 2 * tq
    hq = pl.BlockSpec((N_HEADS, tq, 128), lambda i: (0, i, 0))
    col = lambda j: pl.BlockSpec((s, 128), lambda i: (0, j))
    kern = functools.partial(_dsa_prompt_kernel, tq=tq, tk=tk, topk=topk)
    return pl.pallas_call(
        kern,
        grid=(s // tq,),
        in_specs=[pl.BlockSpec(memory_space=pltpu.SMEM),
                  hq, hq,
                  pl.BlockSpec((tq, 128), lambda i: (i, 1)),
                  col(0), col(1), col(6),
                  pl.BlockSpec(btile.shape, lambda i: (0, 0, 0, 0))],
        out_specs=pl.BlockSpec((tq, BRANCH_W), lambda i: (i, 0)),
        out_shape=jax.ShapeDtypeStruct((s, BRANCH_W), F32),
        scratch_shapes=[pltpu.VMEM((s // LANES, tq, LANES), I32),
                        pltpu.VMEM((N_HEADS * tq, LANES), F32),
                        pltpu.VMEM((N_HEADS * tq, LANES), F32),
                        pltpu.VMEM((N_HEADS * tq, LANES), F32)],
        compiler_params=_cparams(("arbitrary",)),
        name="dsa_prompt",
    )(b31, qi, qa, st, kvb, kvb, kvb, btile)


def _flash_prompt_kernel(q_ref, kv_ref, aux_ref, o_ref, m_scr, l_scr, acc_scr, *, tq, tk, mode):
    i = pl.program_id(0)
    t0 = i * tq
    dq = q_ref.shape[2]
    q = q_ref[...].reshape(N_HEADS * tq, dq)
    nfull = t0 // tk
    _init_state(m_scr, l_scr, acc_scr)

    def kv(k0):
        if mode == "fox":
            return kv_ref[pl.ds(k0, tk), 0:LANES], kv_ref[pl.ds(k0, tk), LANES:2 * LANES]
        k = kv_ref[pl.ds(k0, tk), :]
        return k, k[:, 0:LANES]

    def logits(c, k):
        s = _nt(q, k)
        out = []
        for h in range(N_HEADS):
            sh = s[h * tq:(h + 1) * tq]
            if mode == "fox":
                sh = sh + aux_ref[c, h:h + 1, :]
            out.append(sh)
        return out

    def full_body(c, carry):
        k, v = kv(pl.multiple_of(c * tk, tk))
        _softmax_update(logits(c, k), v, m_scr, l_scr, acc_scr, tq)
        return carry

    lax.fori_loop(0, nfull, full_body, 0)

    k0 = pl.multiple_of(nfull * tk, tk)
    k, v = kv(k0)
    causal = (k0 + lax.broadcasted_iota(I32, (tq, tk), 1)) <= (t0 + lax.broadcasted_iota(I32, (tq, tk), 0))
    s_list = [jnp.where(causal, sh, NEG) for sh in logits(nfull, k)]
    _softmax_update(s_list, v, m_scr, l_scr, acc_scr, tq)

    if mode == "fox":
        lane = lax.broadcasted_iota(I32, (tq, LANES), 1)
        for c in range(N_HEADS // 2):
            r0 = slice(c * tq, (c + 1) * tq)
            r1 = slice((c + 4) * tq, (c + 5) * tq)
            o_ref[:, c * LANES:(c + 1) * LANES] = jnp.where(
                lane < D_HEAD, acc_scr[r0, :] / l_scr[r0, :], acc_scr[r1, :] / l_scr[r1, :])
    else:
        out = None
        for h in range(N_HEADS):
            rows = slice(h * tq, (h + 1) * tq)
            olat = (acc_scr[rows, :] / l_scr[rows, :]).astype(BF16)
            d = jnp.dot(olat, aux_ref[h], preferred_element_type=F32)
            out = d if out is None else out + d
        o_ref[...] = out


def _flash_prompt(q, kvb, aux, mode):
    s = kvb.shape[0]
    tq, tk = 128, min(512, s)
    dq = q.shape[2]
    kern = functools.partial(_flash_prompt_kernel, tq=tq, tk=tk, mode=mode)
    return pl.pallas_call(
        kern,
        grid=(s // tq,),
        in_specs=[pl.BlockSpec((N_HEADS, tq, dq), lambda i: (0, i, 0)),
                  pl.BlockSpec((s, 256), lambda i: (0, 1 if mode == "fox" else 2)),
                  pl.BlockSpec(aux.shape, lambda i: (0, 0, 0))],
        out_specs=pl.BlockSpec((tq, BRANCH_W), lambda i: (i, 0)),
        out_shape=jax.ShapeDtypeStruct((s, BRANCH_W), F32),
        scratch_shapes=[pltpu.VMEM((N_HEADS * tq, LANES), F32),
                        pltpu.VMEM((N_HEADS * tq, LANES), F32),
                        pltpu.VMEM((N_HEADS * tq, LANES), F32)],
        compiler_params=_cparams(("arbitrary",)),
        name=mode + "_prompt",
    )(q, kvb, aux)


def _merge_kernel(oa_ref, ob_ref, oc_ref, z_ref, g_ref, x_ref, wbr_ref, wout_ref, lng_ref, lnb_ref, o_ref,
                  *, alpha):
    mixed = None
    for n, o in enumerate((oa_ref, ob_ref, oc_ref)):
        zn = z_ref[:, n * BRANCH_W:(n + 1) * BRANCH_W]
        u = o[...] * (zn * jax.nn.sigmoid(zn))
        y = jnp.dot(u.astype(BF16), wbr_ref[n], preferred_element_type=F32)
        t = jax.nn.sigmoid(g_ref[:, n * D_MODEL:(n + 1) * D_MODEL]) * y
        mixed = t if mixed is None else mixed + t
    out = jnp.dot(mixed.astype(BF16), wout_ref[...], preferred_element_type=F32)
    hres = alpha * x_ref[...] + out
    mu = jnp.mean(hres, axis=-1, keepdims=True)
    d = hres - mu
    var = jnp.mean(d * d, axis=-1, keepdims=True)
    o_ref[...] = d * lax.rsqrt(var + LN_EPS) * lng_ref[...] + lnb_ref[...]


def _merge(oa, ob, oc, y, x, wbr, wout, lng, lnb, alpha):
    m = x.shape[0]
    tm = min(m, 256)
    row = lambda w: pl.BlockSpec((tm, w), lambda i: (i, 0))
    full = lambda a: pl.BlockSpec(a.shape, lambda i: (0,) * a.ndim)
    return pl.pallas_call(
        functools.partial(_merge_kernel, alpha=alpha),
        grid=(m // tm,),
        in_specs=[row(BRANCH_W), row(BRANCH_W), row(BRANCH_W),
                  pl.BlockSpec((tm, 1536), lambda i: (i, OFF_Z // 1536)),
                  pl.BlockSpec((tm, 3072), lambda i: (i, OFF_G // 3072)),
                  row(D_MODEL), full(wbr), full(wout), full(lng), full(lnb)],
        out_specs=row(D_MODEL),
        out_shape=jax.ShapeDtypeStruct((m, D_MODEL), F32),
        compiler_params=_cparams(("arbitrary",)),
        name="merge",
    )(oa, ob, oc, y, y, x, wbr, wout, lng, lnb)


def _np_consts():
    sela = np.zeros((512, 1024), np.float32)
    selb = np.zeros((512, 1024), np.float32)
    for h in range(N_HEADS):
        for d in range(D_HEAD):
            sela[h * 64 + d, h * 128 + d] = 1.0
            selb[h * 64 + d, h * 128 + (0 if h < 4 else 64) + d] = 1.0
    psel = np.zeros((256, 1024), np.float32)
    for h in range(N_HEADS):
        for j in range(16):
            psel[h * 16 + j, h * 128 + j] = 1.0
            psel[128 + h * 16 + j, h * 128 + 16 + j] = 1.0
    i = np.arange(1024)
    r, h = i // 8, i % 8
    hp, rp = i // 128, i % 128
    m2 = (h[:, None] == hp[None, :])
    m1 = m2 & (r[:, None] <= rp[None, :])
    return sela, selb, psel, m1.astype(np.float32), m2.astype(np.float32)


def _t5_bucket(dist):
    max_exact = N_BUCKETS // 2
    d = jnp.maximum(dist, 1).astype(F32)
    large = max_exact + (jnp.log(d / max_exact) / np.log(MAX_DISTANCE / max_exact)
                         * (N_BUCKETS - max_exact)).astype(I32)
    large = jnp.minimum(large, N_BUCKETS - 1)
    return jnp.where(dist < max_exact, dist, large)


def _rope_tables(pos):
    half = D_ROPE // 2
    freqs = ROPE_THETA ** (-jnp.arange(half, dtype=F32) / half)
    ang = pos.astype(F32)[:, None] * freqs
    cos, sin = jnp.cos(ang), jnp.sin(ang)
    m = pos.shape[0]
    cosq, sinq = jnp.tile(cos, (1, N_HEADS)), jnp.tile(sin, (1, N_HEADS))
    cosk = jnp.ones((m, LANES), F32).at[:, L_KROPE:L_KROPE + 16].set(cos).at[:, L_KROPE + 16:L_KROPE + 32].set(cos)
    sink = jnp.zeros((m, LANES), F32).at[:, L_KROPE:L_KROPE + 16].set(-sin).at[:, L_KROPE + 16:L_KROPE + 32].set(sin)
    return cosq, sinq, cosk, sink


def _prep_layer(l, w_in, b_forget, w_c_uq, g_c_q, g_c_kv, w_c_ukv, w_branch, w_out, ln_g, ln_b):
    sela, selb, psel, _, _ = _np_consts()
    src, scale = _col_perm()
    wp = (jnp.take(w_in[l], jnp.asarray(src), axis=1) * jnp.asarray(scale)).astype(BF16)
    bf_row = jnp.zeros((1, LANES), F32).at[0, L_BF:L_BF + N_HEADS].set(b_forget[l])
    wq = w_c_uq[l]
    wq2 = jnp.concatenate([wq[:, :, :D_NOPE].reshape(D_CQ, 512),
                           wq[:, :, D_NOPE:D_NOPE + 16].reshape(D_CQ, 128),
                           wq[:, :, D_NOPE + 16:].reshape(D_CQ, 128)], axis=1).astype(BF16)
    w_uk, w_uv = w_c_ukv[l][..., :D_NOPE], w_c_ukv[l][..., D_NOPE:]
    eye = jnp.eye(N_HEADS, dtype=F32)
    wukbd = jnp.einsum('chn,hg->hngc', w_uk, eye).reshape(512, 1024).astype(BF16)
    wuvpad = jnp.einsum('chv,hg->hcgv', w_uv, eye).reshape(N_HEADS, D_LATENT, 512).astype(BF16)
    wbr = w_branch[l].at[1].set(w_branch[l][1][jnp.asarray(_fox_head_perm())]).astype(BF16)
    return dict(
        wp=wp, wuvpad=wuvpad, wbr=wbr, wout=w_out[l].astype(BF16), lng=ln_g[l][None], lnb=ln_b[l][None],
        post=(bf_row, g_c_q[l][None], g_c_kv[l][None], jnp.asarray(sela, BF16), jnp.asarray(selb, BF16),
              wq2, wukbd, jnp.asarray(psel, BF16)))


def _bias_tables(t5_table):
    r = jnp.arange(PAGE)[:, None]
    c = jnp.arange(PAGE)[None, :]
    dist = jnp.stack([PAGE + r - c, jnp.maximum(r - c, 0)])
    btile = jnp.transpose(t5_table[_t5_bucket(dist)], (0, 3, 1, 2))
    bfar = t5_table[_t5_bucket(jnp.asarray(2 * PAGE))]
    return btile.astype(F32), bfar.astype(F32)


def _prompt_layer(xp, prep, tabs, btile, bfar, topk, alpha, m1, m2):
    s = xp.shape[0]
    y = _project(xp, prep['wp'])
    qa, qi, qb, qc, st, kvb = _post(y, tabs, prep['post'])
    nb = s // PAGE
    wt = _page_prefix(st[:, 128 + L_BF:128 + L_BF + N_HEADS].reshape(nb, 1024), m1, m2)
    ltri = jnp.asarray(np.tril(np.ones((nb, nb), np.float32), -1), BF16)
    nf = _blk_prefix(wt, ltri)
    tk = min(512, s)
    nf = jnp.transpose(nf.reshape(nb, N_HEADS, PAGE), (1, 0, 2)).reshape(N_HEADS, s // tk, tk)
    nf = jnp.transpose(nf, (1, 0, 2))
    o_a = _dsa_prompt(bfar, qi, qa, st, kvb, btile, topk)
    o_b = _flash_prompt(qb, kvb, nf, "fox")
    o_c = _flash_prompt(qc, kvb, prep['wuvpad'], "mla")
    xp = _merge(o_a, o_b, o_c, y, xp, prep['wbr'], prep['wout'], prep['lng'], prep['lnb'], alpha)
    return xp, st


def _state_rows(st, lead):
    f = lambda a, *tail: a.reshape(lead + tail)
    return (f(st[:, 0:64], 64), f(st[:, 64:128], 64), f(st[:, 128:192], 64),
            f(st[:, 256:384], KV_B, D_HEAD), f(st[:, 384:512], KV_B, D_HEAD),
            f(st[:, 128 + L_BF:128 + L_BF + N_HEADS], N_HEADS),
            f(st[:, 512:640], D_LATENT), f(st[:, 128 + L_KROPE:128 + L_KROPE + D_ROPE], D_ROPE))


def _page_specs(page_shape, layer, p_per_step):
    nd = len(page_shape)
    return [pl.BlockSpec((None, None) + page_shape,
                         lambda b, c, pt, p=p: (layer, pt[b, c * p_per_step + p]) + (0,) * nd)
            for p in range(p_per_step)]


def _pages_t(cache):
    nd = cache.ndim
    return jnp.transpose(cache, (0, 1) + tuple(range(3, nd)) + (2,))


def _seq_spec(*shape):
    nd = len(shape)
    return pl.BlockSpec((None,) + shape, lambda b, c, pt: (b,) + (0,) * nd)


def _tile_lanes(x, n):
    return jnp.concatenate([x] * n, axis=1) if n > 1 else x


def _dec_update(s, pv_fn, m_ref, l_ref, acc_ref):
    dv = acc_ref.shape[1]
    m_prev = m_ref[...]
    m_new = jnp.maximum(m_prev, jnp.max(s, axis=1, keepdims=True))
    alpha = jnp.exp(m_prev - m_new)
    p = jnp.exp(s - _tile_lanes(m_new, s.shape[1] // LANES))
    l_ref[...] = alpha * l_ref[...] + jnp.sum(p, axis=1, keepdims=True)
    m_ref[...] = m_new
    acc_ref[...] = alpha[:, 0:dv] * acc_ref[...] + pv_fn(p.astype(BF16))


def _dec_self(s_self, v_self, m_ref, l_ref, acc_ref):
    dv = acc_ref.shape[1]
    m_prev = m_ref[...]
    m_new = jnp.maximum(m_prev, s_self)
    alpha = jnp.exp(m_prev - m_new)
    p = jnp.exp(s_self - m_new)
    l_ref[...] = alpha * l_ref[...] + p
    m_ref[...] = m_new
    acc_ref[...] = alpha[:, 0:dv] * acc_ref[...] + p[:, 0:dv] * v_self


def _idx_sample_kernel(pt_ref, q_ref, w_ref, g1_ref, *rest, pps):
    pages, sc_ref, self_ref = rest[:pps], rest[pps], rest[pps + 1]
    q = q_ref[...]
    w = w_ref[...]
    kt = jnp.concatenate([p[...] for p in pages], axis=1).astype(BF16)
    y = jnp.dot(q[:, 0:D_HEAD], kt, preferred_element_type=F32)
    sc_ref[...] = jnp.sum(jnp.maximum(y, 0.0) * _tile_lanes(w, pps), axis=0, keepdims=True)

    @pl.when(pl.program_id(1) == 0)
    def _():
        ys = jnp.sum(q.astype(F32) * g1_ref[...], axis=1, keepdims=True)
        val = jnp.sum(jnp.maximum(ys, 0.0) * w[:, 0:1], axis=0, keepdims=True)
        lane = lax.broadcasted_iota(I32, (1, LANES), 1)
        self_ref[...] = jnp.where(lane == 0, val, -jnp.inf)


def _idx_sample(page_table, layer, qi_t, w8, g1new, cache_kidx, pps):
    b, npages = page_table.shape
    nch = npages // pps
    grid_spec = pltpu.PrefetchScalarGridSpec(
        num_scalar_prefetch=1, grid=(b, nch),
        in_specs=[_seq_spec(N_HEADS, LANES), _seq_spec(N_HEADS, LANES), _seq_spec(1, LANES)]
        + _page_specs((D_HEAD, PAGE), layer, pps),
        out_specs=[pl.BlockSpec((None, 1, PAGE * pps), lambda b, c, pt: (b, 0, c)), _seq_spec(1, LANES)])
    return pl.pallas_call(
        functools.partial(_idx_sample_kernel, pps=pps), grid_spec=grid_spec,
        out_shape=[jax.ShapeDtypeStruct((b, 1, npages * PAGE), F32), jax.ShapeDtypeStruct((b, 1, LANES), F32)],
        compiler_params=_cparams(("arbitrary", "arbitrary")), name="idx_sample",
    )(page_table, qi_t, w8, g1new, *([cache_kidx] * pps))


def _thresh_kernel(sc_ref, o_ref, key_scr, *, topk):
    rows = sc_ref.shape[0]
    nsl = sc_ref.shape[1] // LANES
    for j in range(nsl):
        key_scr[j] = _float_key(sc_ref[:, j * LANES:(j + 1) * LANES])

    def count_ge(cand):
        cnt = lax.fori_loop(0, nsl, lambda j, cnt: cnt + (key_scr[j] >= cand).astype(I32),
                            jnp.zeros((rows, LANES), I32))
        return jnp.sum(cnt.astype(F32), axis=1, keepdims=True)

    thr = _kth_largest_key(count_ge, float(topk), (rows, LANES))
    bits = thr ^ ((thr >> 31) & 0x7FFFFFFF)
    o_ref[...] = jnp.where(thr == INT_MIN, -jnp.inf, pltpu.bitcast(bits, F32))


def _thresh(sc, topk):
    rows, n = sc.shape
    return pl.pallas_call(
        functools.partial(_thresh_kernel, topk=topk), grid=(1,),
        in_specs=[pl.BlockSpec((rows, n), lambda i: (0, 0))],
        out_specs=pl.BlockSpec((rows, LANES), lambda i: (0, 0)),
        out_shape=jax.ShapeDtypeStruct((rows, LANES), F32),
        scratch_shapes=[pltpu.VMEM((n // LANES, rows, LANES), I32)],
        compiler_params=_cparams(("arbitrary",)), name="thresh_sample",
    )(sc)


def _dsa_sample_kernel(pt_ref, q_ref, sc_ref, self_ref, thr_ref, g0_ref, bfar_ref, blast_ref, bself_ref, *rest,
                       pps):
    kp, vp = rest[:pps], rest[pps:2 * pps]
    o_ref, m_scr, l_scr, acc_scr = rest[2 * pps:]
    c = pl.program_id(1)
    last = c == pl.num_programs(1) - 1

    @pl.when(c == 0)
    def _():
        _init_state(m_scr, l_scr, acc_scr)

    q = q_ref[...]
    kt = jnp.concatenate([p[...] for p in kp], axis=1).astype(BF16)
    vt = jnp.concatenate([p[...] for p in vp], axis=1).astype(BF16)
    thr = thr_ref[...]
    sel = sc_ref[...] >= _tile_lanes(thr, pps)
    bfar = bfar_ref[...]
    bias = jnp.concatenate([bfar] * (pps - 1) + [jnp.where(last, blast_ref[...], bfar)], axis=1)
    s = jnp.where(sel, jnp.dot(q[:, 0:D_HEAD], kt, preferred_element_type=F32) + bias, NEG)
    _dec_update(s, lambda p: _nt(p, vt), m_scr, l_scr, acc_scr)

    @pl.when(last)
    def _():
        g0 = g0_ref[...]
        s_self = jnp.sum(q.astype(F32) * g0, axis=1, keepdims=True) + bself_ref[...]
        s_self = jnp.where(self_ref[:, 0:1] >= thr[:, 0:1], s_self, NEG)
        _dec_self(s_self, g0[:, D_HEAD:2 * D_HEAD], m_scr, l_scr, acc_scr)
        o_ref[...] = acc_scr[...] / l_scr[:, 0:D_HEAD]


def _dsa_sample(page_table, layer, qa_t, sc, selfsc, thr, g0new, bfar8, blast, bself8, cache_k, cache_v, pps):
    b, npages = page_table.shape
    full = lambda a: pl.BlockSpec(a.shape, lambda b, c, pt: (0,) * a.ndim)
    grid_spec = pltpu.PrefetchScalarGridSpec(
        num_scalar_prefetch=1, grid=(b, npages // pps),
        in_specs=[_seq_spec(N_HEADS, LANES), pl.BlockSpec((None, 1, PAGE * pps), lambda b, c, pt: (b, 0, c)),
                  _seq_spec(1, LANES), _seq_spec(1, LANES), _seq_spec(1, LANES), full(bfar8), full(blast), full(bself8)]
        + _page_specs((D_HEAD, PAGE), layer, pps) + _page_specs((D_HEAD, PAGE), layer, pps),
        out_specs=_seq_spec(N_HEADS, D_HEAD),
        scratch_shapes=[pltpu.VMEM((N_HEADS, LANES), F32), pltpu.VMEM((N_HEADS, LANES), F32),
                        pltpu.VMEM((N_HEADS, D_HEAD), F32)])
    return pl.pallas_call(
        functools.partial(_dsa_sample_kernel, pps=pps), grid_spec=grid_spec,
        out_shape=jax.ShapeDtypeStruct((b, N_HEADS, D_HEAD), F32),
        compiler_params=_cparams(("arbitrary", "arbitrary")), name="dsa_sample",
    )(page_table, qa_t, sc, selfsc, thr, g0new, bfar8, blast, bself8, *([cache_k] * pps), *([cache_v] * pps))


def _fox_sample_kernel(pt_ref, q_ref, st_ref, lf_ref, utri_ref, ones_ref, *rest, pps):
    kp, vp, fp = rest[:pps], rest[pps:2 * pps], rest[2 * pps:3 * pps]
    o_ref, m_scr, l_scr, acc_scr, carry_scr = rest[3 * pps:]
    c = pl.program_id(1)

    @pl.when(c == 0)
    def _():
        _init_state(m_scr, l_scr, acc_scr)
        carry_scr[...] = jnp.zeros(carry_scr.shape, F32)

    q = q_ref[...]
    q0, q1 = q[:, 0:D_HEAD], q[:, D_HEAD:2 * D_HEAD]
    kv = lambda pages, g: jnp.concatenate([p[g] for p in pages], axis=1).astype(BF16)
    s = (jnp.dot(q0, kv(kp, 0), preferred_element_type=F32)
         + jnp.dot(q1, kv(kp, 1), preferred_element_type=F32))
    parts = _split3(jnp.concatenate([p[...] for p in fp], axis=0))
    wsum = tsum = None
    for part in parts:
        dw = jnp.dot(part, utri_ref[...], preferred_element_type=F32)
        dt = jnp.dot(part, ones_ref[...], preferred_element_type=F32)
        wsum = dw if wsum is None else wsum + dw
        tsum = dt if tsum is None else tsum + dt
    po = carry_scr[...]
    biases = []
    for p in range(pps):
        biases.append(wsum[p * N_HEADS:(p + 1) * N_HEADS, :] + po)
        po = po + tsum[p * N_HEADS:(p + 1) * N_HEADS, :]
    carry_scr[...] = po
    s = s - jnp.concatenate(biases, axis=1)
    v0, v1 = kv(vp, 0), kv(vp, 1)
    row = lax.broadcasted_iota(I32, (N_HEADS, D_HEAD), 0)
    lower = row < N_HEADS // KV_B
    _dec_update(s, lambda p: jnp.where(lower, _nt(p, v0), _nt(p, v1)), m_scr, l_scr, acc_scr)

    @pl.when(c == pl.num_programs(1) - 1)
    def _():
        k_new, v_new = st_ref[:, 256:384], st_ref[:, 384:512]
        s_self = jnp.sum(q.astype(F32) * k_new, axis=1, keepdims=True) - (po + lf_ref[...])
        v_self = jnp.where(lower, jnp.broadcast_to(v_new[:, 0:D_HEAD], (N_HEADS, D_HEAD)),
                           jnp.broadcast_to(v_new[:, D_HEAD:2 * D_HEAD], (N_HEADS, D_HEAD)))
        _dec_self(s_self, v_self, m_scr, l_scr, acc_scr)
        o_ref[...] = acc_scr[...] / l_scr[:, 0:D_HEAD]


def _fox_sample(page_table, layer, qb_t, st3, lf8, cache_k, cache_v, cache_f, pps):
    b, npages = page_table.shape
    utri = jnp.asarray(np.triu(np.ones((PAGE, PAGE), np.float32)), BF16)
    ones = jnp.ones((PAGE, PAGE), BF16)
    full = lambda a: pl.BlockSpec(a.shape, lambda b, c, pt: (0,) * a.ndim)
    grid_spec = pltpu.PrefetchScalarGridSpec(
        num_scalar_prefetch=1, grid=(b, npages // pps),
        in_specs=[_seq_spec(N_HEADS, LANES), _seq_spec(1, 640), _seq_spec(N_HEADS, LANES), full(utri), full(ones)]
        + _page_specs((KV_B, D_HEAD, PAGE), layer, pps) + _page_specs((KV_B, D_HEAD, PAGE), layer, pps)
        + _page_specs((N_HEADS, PAGE), layer, pps),
        out_specs=_seq_spec(N_HEADS, D_HEAD),
        scratch_shapes=[pltpu.VMEM((N_HEADS, LANES), F32), pltpu.VMEM((N_HEADS, LANES), F32),
                        pltpu.VMEM((N_HEADS, D_HEAD), F32), pltpu.VMEM((N_HEADS, LANES), F32)])
    return pl.pallas_call(
        functools.partial(_fox_sample_kernel, pps=pps), grid_spec=grid_spec,
        out_shape=jax.ShapeDtypeStruct((b, N_HEADS, D_HEAD), F32),
        compiler_params=_cparams(("arbitrary", "arbitrary")), name="fox_sample",
    )(page_table, qb_t, st3, lf8, utri, ones, *([cache_k] * pps), *([cache_v] * pps), *([cache_f] * pps))


def _mla_sample_kernel(pt_ref, q_ref, kn_ref, *rest, pps):
    cp, rp = rest[:pps], rest[pps:2 * pps]
    o_ref, m_scr, l_scr, acc_scr = rest[2 * pps:]
    c = pl.program_id(1)

    @pl.when(c == 0)
    def _():
        _init_state(m_scr, l_scr, acc_scr)

    q = q_ref[...]
    ckv = jnp.concatenate([p[...] for p in cp], axis=0).astype(BF16)
    krt = jnp.concatenate([p[...] for p in rp], axis=1).astype(BF16)
    s = _nt(q[:, 0:D_LATENT], ckv) + jnp.dot(q[:, D_LATENT:D_LATENT + D_ROPE], krt, preferred_element_type=F32)
    _dec_update(s, lambda p: jnp.dot(p, ckv, preferred_element_type=F32), m_scr, l_scr, acc_scr)

    @pl.when(c == pl.num_programs(1) - 1)
    def _():
        kn = kn_ref[...].astype(F32)
        s_self = jnp.sum(q.astype(F32) * kn, axis=1, keepdims=True)
        _dec_self(s_self, kn[:, 0:D_LATENT], m_scr, l_scr, acc_scr)
        o_ref[...] = acc_scr[...] / l_scr[...]


def _mla_sample(page_table, layer, qc_t, kvb3, cache_lat, cache_kr, pps):
    b, npages = page_table.shape
    grid_spec = pltpu.PrefetchScalarGridSpec(
        num_scalar_prefetch=1, grid=(b, npages // pps),
        in_specs=[_seq_spec(N_HEADS, 2 * LANES), pl.BlockSpec((None, 1, 2 * LANES), lambda b, c, pt: (b, 0, 2))]
        + _page_specs((PAGE, D_LATENT), layer, pps) + _page_specs((D_ROPE, PAGE), layer, pps),
        out_specs=_seq_spec(N_HEADS, D_LATENT),
        scratch_shapes=[pltpu.VMEM((N_HEADS, LANES), F32), pltpu.VMEM((N_HEADS, LANES), F32),
                        pltpu.VMEM((N_HEADS, D_LATENT), F32)])
    return pl.pallas_call(
        functools.partial(_mla_sample_kernel, pps=pps), grid_spec=grid_spec,
        out_shape=jax.ShapeDtypeStruct((b, N_HEADS, D_LATENT), F32),
        compiler_params=_cparams(("arbitrary", "arbitrary")), name="mla_sample",
    )(page_table, qc_t, kvb3, *([cache_lat] * pps), *([cache_kr] * pps))


def _uv_kernel(o_ref, w_ref, out_ref):
    out = None
    for h in range(N_HEADS):
        d = jnp.dot(o_ref[h].astype(BF16), w_ref[h], preferred_element_type=F32)
        out = d if out is None else out + d
    out_ref[...] = out


def _uv_proj(olat_t, wuvpad):
    b = olat_t.shape[1]
    return pl.pallas_call(
        _uv_kernel, grid=(1,),
        in_specs=[pl.BlockSpec(olat_t.shape, lambda i: (0, 0, 0)), pl.BlockSpec(wuvpad.shape, lambda i: (0, 0, 0))],
        out_specs=pl.BlockSpec((b, BRANCH_W), lambda i: (0, 0)),
        out_shape=jax.ShapeDtypeStruct((b, BRANCH_W), F32),
        compiler_params=_cparams(("arbitrary",)), name="uv_proj",
    )(olat_t, wuvpad)


def _sample_layer(xs, layer, prep, tabs, caches, page_table, bias_s, topk, alpha, pps):
    ca_k, ca_v, ca_kidx, cb_k, cb_v, cb_f, cc_lat, cc_kr = caches
    bfar8, blast, bself8 = bias_s
    b = xs.shape[0]
    y = _project(xs, prep['wp'])
    qa, qi, qb, qc, st, kvb = _post(y, tabs, prep['post'])
    tr = lambda q: jnp.transpose(q, (1, 0, 2))
    g1 = st[:, 128:256]
    bc8 = lambda a: jnp.broadcast_to(a[:, :, None], (b, N_HEADS, LANES))
    w8 = bc8(g1[:, L_WIDX:L_WIDX + N_HEADS])
    lf8 = bc8(g1[:, L_BF:L_BF + N_HEADS])
    st3 = st[:, None, :]
    sc, selfsc = _idx_sample(page_table, layer, tr(qi), w8, g1[:, None, :], ca_kidx, pps)
    thr = _thresh(jnp.concatenate([sc[:, 0, :], selfsc[:, 0, :]], axis=1), topk)
    o_a = _dsa_sample(page_table, layer, tr(qa), sc, selfsc, thr[:, None, :], st3[:, :, 0:128], bfar8, blast, bself8,
                      ca_k, ca_v, pps)
    o_b = _fox_sample(page_table, layer, tr(qb), st3, lf8, cb_k, cb_v, cb_f, pps)
    olat = _mla_sample(page_table, layer, tr(qc), kvb[:, None, :], cc_lat, cc_kr, pps)
    o_c = _uv_proj(tr(olat), prep['wuvpad'])
    o_b = o_b.reshape(b, BRANCH_W)[:, jnp.asarray(_fox_head_perm())]
    xs = _merge(o_a.reshape(b, BRANCH_W), o_b, o_c, y, xs, prep['wbr'], prep['wout'], prep['lng'], prep['lnb'], alpha)
    return xs, st


def kernel(x_prompt, x_sample, cache_a_k, cache_a_v, cache_a_kidx, cache_b_k, cache_b_v, cache_b_logf,
           cache_c_latent, cache_c_krope, page_table, t5_table, w_in, b_forget, w_c_uq, g_c_q, g_c_kv,
           w_c_ukv, w_branch, w_out, ln_g, ln_b):
    depth = w_in.shape[0]
    bp, s, _ = x_prompt.shape
    bs, ds, _ = x_sample.shape
    npages = page_table.shape[1]
    past = npages * PAGE
    n_pool = cache_a_k.shape[1]
    assert bp == 1 and ds == 1 and cache_a_k.shape[2] == PAGE
    alpha = float((2 * depth) ** 0.25)
    pps = 8 if npages % 8 == 0 and npages >= 16 else npages // 2
    _, _, _, m1, m2 = _np_consts()
    m1, m2 = jnp.asarray(m1, BF16), jnp.asarray(m2, BF16)
    btile, bfar = _bias_tables(t5_table)
    bc = lambda a: jnp.broadcast_to(a[:, None], (N_HEADS, LANES)).astype(F32)
    dist_last = PAGE - jnp.arange(PAGE)
    bias_s = (bc(bfar), jnp.transpose(t5_table[_t5_bucket(dist_last)]).astype(F32),
              bc(t5_table[_t5_bucket(jnp.asarray(0))]))
    tabs_p = _rope_tables(jnp.arange(s, dtype=I32))
    tabs_s = _rope_tables(jnp.full((bs,), past, I32))
    caches = (_pages_t(cache_a_k), _pages_t(cache_a_v), _pages_t(cache_a_kidx), _pages_t(cache_b_k),
              _pages_t(cache_b_v), _pages_t(cache_b_logf), cache_c_latent, _pages_t(cache_c_krope))
    xp = x_prompt.reshape(s, D_MODEL)
    xs = x_sample.reshape(bs, D_MODEL)
    rows_p, rows_s = [], []
    for l in range(depth):
        prep = _prep_layer(l, w_in, b_forget, w_c_uq, g_c_q, g_c_kv, w_c_ukv, w_branch, w_out, ln_g, ln_b)
        xp, st_p = _prompt_layer(xp, prep, tabs_p, btile, bfar, min(TOPK_MAX, s // 4), alpha, m1, m2)
        rows_p.append(_state_rows(st_p, (bp, s)))
        xs, st_s = _sample_layer(xs, l, prep, tabs_s, caches, page_table, bias_s,
                                 min(TOPK_MAX, (past + 1) // 4), alpha, pps)
        rows_s.append(_state_rows(st_s, (bs, 1)))
    new_p = [jnp.stack([r[i] for r in rows_p], axis=0) for i in range(8)]
    new_s = [jnp.stack([r[i] for r in rows_s], axis=0) for i in range(8)]
    return (xp.reshape(bp, s, D_MODEL), xs.reshape(bs, 1, D_MODEL), *new_p, *new_s)
```

```python
import functools

import numpy as np
import jax
import jax.numpy as jnp
from jax import lax
from jax.experimental import pallas as pl
from jax.experimental.pallas import tpu as pltpu

F32 = jnp.float32
BF16 = jnp.bfloat16
I32 = jnp.int32
I16 = jnp.int16

N_HEADS = 8
D_HEAD = 64
KV_B = 2
D_NOPE = 64
D_ROPE = 32
D_CQ = 256
D_LATENT = 128
TOPK_MAX = 256
ROPE_THETA = 10000.0
N_BUCKETS = 32
MAX_DISTANCE = 128
BRANCH_W = 512
D_MODEL = 1024
PAGE = 128
LN_EPS = 1e-5
RMS_EPS = 1e-6

LANES = 128
NEG = -1e30
LOG2E = 1.4426950408889634
INT_MIN = int(np.iinfo(np.int32).min)
VMEM_LIMIT = 56 * 1024 * 1024

SRC_SPLITS = (
    ('a_q', 512), ('a_k', 64), ('a_v', 64), ('a_qidx', 512), ('a_kidx', 64), ('a_widx', 8), ('a_z', 512),
    ('b_q', 512), ('b_k', 128), ('b_v', 128), ('b_f', 8), ('b_z', 512),
    ('c_q', 256), ('c_kv', 128), ('c_krope', 32), ('c_z', 512), ('gates', 3072))
NP = 7168
OFF_Q = 0
OFF_Z = 1536
OFF_G = 3072
OFF_S = 6144
L_KROPE, L_BF, L_WIDX = 64, 96, 104


def _fox_head_perm():
    j = np.arange(BRANCH_W)
    return ((j // 128) + 4 * ((j % 128) // 64)) * 64 + (j % 64)


def _col_perm():
    src_off, o = {}, 0
    for name, n in SRC_SPLITS:
        src_off[name] = o
        o += n
    src = -np.ones((NP,), np.int64)
    scale = np.ones((NP,), np.float32)

    def put(dst, name, n, perm=None):
        idx = np.arange(n) if perm is None else perm
        src[dst:dst + n] = src_off[name] + idx

    put(0, 'a_q', 512)
    scale[0:512] = D_HEAD ** -0.5
    put(512, 'a_qidx', 512)
    put(1024, 'b_q', 512)
    scale[1024:1536] = D_HEAD ** -0.5
    put(OFF_Z, 'a_z', 512)
    put(OFF_Z + 512, 'b_z', 512, _fox_head_perm())
    put(OFF_Z + 1024, 'c_z', 512)
    put(OFF_G, 'gates', 3072)
    put(OFF_S, 'c_q', 256)
    g0 = OFF_S + 256
    put(g0, 'a_k', 64)
    put(g0 + 64, 'a_v', 64)
    g1 = g0 + 128
    put(g1, 'a_kidx', 64)
    put(g1 + L_KROPE, 'c_krope', 32)
    put(g1 + L_BF, 'b_f', 8)
    put(g1 + L_WIDX, 'a_widx', 8)
    put(g1 + 128, 'b_k', 128)
    put(g1 + 256, 'b_v', 128)
    put(g1 + 384, 'c_kv', 128)
    scale[src < 0] = 0.0
    return np.maximum(src, 0), scale


def _split3(x):
    hi = x.astype(BF16)
    r = x - hi.astype(F32)
    mid = r.astype(BF16)
    lo = (r - mid.astype(F32)).astype(BF16)
    return lo, mid, hi


def _nt(a, b):
    return lax.dot_general(a, b, (((1,), (1,)), ((), ())), preferred_element_type=F32)


def _cparams(sem):
    return pltpu.CompilerParams(dimension_semantics=sem, vmem_limit_bytes=VMEM_LIMIT)


def _mm_kernel(x_ref, w_ref, o_ref, xb_ref):
    @pl.when(pl.program_id(1) == 0)
    def _():
        xb_ref[...] = x_ref[...].astype(BF16)

    o_ref[...] = jnp.dot(xb_ref[...], w_ref[...], preferred_element_type=F32)


def _project(x, wp):
    m, k = x.shape
    n = wp.shape[1]
    tm = min(m, 1024)
    tn = 1024
    return pl.pallas_call(
        _mm_kernel,
        grid=(m // tm, n // tn),
        in_specs=[pl.BlockSpec((tm, k), lambda i, j: (i, 0)),
                  pl.BlockSpec((k, tn), lambda i, j: (0, j))],
        out_specs=pl.BlockSpec((tm, tn), lambda i, j: (i, j)),
        out_shape=jax.ShapeDtypeStruct((m, n), F32),
        scratch_shapes=[pltpu.VMEM((tm, k), BF16)],
        compiler_params=_cparams(("arbitrary", "arbitrary")),
        name="proj_mm",
    )(x, wp)


def _post_kernel(yq_ref, ys_ref, cosq_ref, sinq_ref, cosk_ref, sink_ref, bf_ref, gq_ref, gkv_ref,
                 sela_ref, selb_ref, wq2_ref, wuk_ref, psel_ref,
                 qa_ref, qi_ref, qb_ref, qc_ref, st_ref, kvb_ref, *, scale_c):
    yq = yq_ref[...]

    def heads(x, sel_ref, out_ref):
        full = jnp.dot(x.astype(BF16), sel_ref[...], preferred_element_type=F32)
        for h in range(N_HEADS):
            out_ref[h] = full[:, h * LANES:(h + 1) * LANES].astype(BF16)

    heads(yq[:, 0:512] * LOG2E, sela_ref, qa_ref)
    heads(yq[:, 512:1024], sela_ref, qi_ref)
    heads(yq[:, 1024:1536] * LOG2E, selb_ref, qb_ref)

    ys = ys_ref[...]
    cq = ys[:, 0:D_CQ]
    cqn = cq * lax.rsqrt(jnp.mean(cq * cq, axis=-1, keepdims=True) + RMS_EPS) * gq_ref[...]
    quq = jnp.dot(cqn.astype(BF16), wq2_ref[...], preferred_element_type=F32)
    x1, x2 = quq[:, 512:640], quq[:, 640:768]
    cos, sin = cosq_ref[...], sinq_ref[...]
    rq = jnp.concatenate([x1 * cos - x2 * sin, x1 * sin + x2 * cos], axis=1) * scale_c
    rsel = jnp.dot(rq.astype(BF16), psel_ref[...], preferred_element_type=F32)
    qlat = jnp.dot(quq[:, 0:512].astype(BF16), wuk_ref[...], preferred_element_type=F32) * scale_c
    for h in range(N_HEADS):
        qc_ref[h, :, 0:LANES] = qlat[:, h * LANES:(h + 1) * LANES].astype(BF16)
        qc_ref[h, :, LANES:2 * LANES] = rsel[:, h * LANES:(h + 1) * LANES].astype(BF16)

    g0 = ys[:, 256:384]
    g1 = ys[:, 384:512]
    g2 = ys[:, 512:640]
    g3 = ys[:, 640:768]
    ckv = ys[:, 768:896]
    ckvn = ckv * lax.rsqrt(jnp.mean(ckv * ckv, axis=-1, keepdims=True) + RMS_EPS) * gkv_ref[...]
    lane = lax.broadcasted_iota(I32, g1.shape, 1)
    swap = jnp.where(lane < L_KROPE + 16, pltpu.roll(g1, LANES - 16, 1), pltpu.roll(g1, 16, 1))
    roped = g1 * cosk_ref[...] + swap * sink_ref[...]
    v = g1 + bf_ref[...]
    logsig = jnp.minimum(v, 0.0) - jnp.log1p(jnp.exp(-jnp.abs(v)))
    g1p = jnp.where((lane >= L_BF) & (lane < L_BF + N_HEADS), logsig, roped)
    v2 = jnp.where(lane < D_HEAD, pltpu.roll(g0, D_HEAD, 1), g0)

    st_ref[:, 0:128] = g0
    st_ref[:, 128:256] = g1p
    st_ref[:, 256:384] = g2
    st_ref[:, 384:512] = g3
    st_ref[:, 512:640] = ckvn
    kvb_ref[:, 0:128] = g0.astype(BF16)
    kvb_ref[:, 128:256] = g1p.astype(BF16)
    kvb_ref[:, 256:384] = g2.astype(BF16)
    kvb_ref[:, 384:512] = g3.astype(BF16)
    kvb_ref[:, 512:640] = ckvn.astype(BF16)
    kvb_ref[:, 640:768] = pltpu.roll(g1p, LANES - L_KROPE, 1).astype(BF16)
    kvb_ref[:, 768:896] = v2.astype(BF16)
    kvb_ref[:, 896:1024] = jnp.zeros_like(g0).astype(BF16)


def _post(y, tabs, consts):
    m = y.shape[0]
    tm = min(m, 256)
    row = lambda w: pl.BlockSpec((tm, w), lambda i: (i, 0))
    full = lambda a: pl.BlockSpec(a.shape, lambda i: (0,) * a.ndim)
    cosq, sinq, cosk, sink = tabs
    bf_row, gq, gkv, sela, selb, wq2, wuk, psel = consts
    hq = lambda w: pl.BlockSpec((N_HEADS, tm, w), lambda i: (0, i, 0))
    kern = functools.partial(_post_kernel, scale_c=float((D_NOPE + D_ROPE) ** -0.5 * LOG2E))
    return pl.pallas_call(
        kern,
        grid=(m // tm,),
        in_specs=[pl.BlockSpec((tm, 1536), lambda i: (i, 0)),
                  pl.BlockSpec((tm, 1024), lambda i: (i, OFF_S // 1024)),
                  row(128), row(128), row(128), row(128),
                  full(bf_row), full(gq), full(gkv), full(sela), full(selb), full(wq2), full(wuk), full(psel)],
        out_specs=[hq(128), hq(128), hq(128), hq(256), row(640), row(1024)],
        out_shape=[jax.ShapeDtypeStruct((N_HEADS, m, 128), BF16),
                   jax.ShapeDtypeStruct((N_HEADS, m, 128), BF16),
                   jax.ShapeDtypeStruct((N_HEADS, m, 128), BF16),
                   jax.ShapeDtypeStruct((N_HEADS, m, 256), BF16),
                   jax.ShapeDtypeStruct((m, 640), F32),
                   jax.ShapeDtypeStruct((m, 1024), BF16)],
        compiler_params=_cparams(("arbitrary",)),
        name="post_proj",
    )(y, y, cosq, sinq, cosk, sink, bf_row, gq, gkv, sela, selb, wq2, wuk, psel)


def _page_prefix_kernel(x_ref, m1_ref, m2_ref, o_ref):
    parts = _split3(x_ref[...])
    w = None
    t = None
    for p in parts:
        dw = jnp.dot(p, m1_ref[...], preferred_element_type=F32)
        dt = jnp.dot(p, m2_ref[...], preferred_element_type=F32)
        w = dw if w is None else w + dw
        t = dt if t is None else t + dt
    o_ref[:, 0:1024] = w
    o_ref[:, 1024:2048] = t


def _page_prefix(x, m1, m2):
    r = x.shape[0]
    tr = min(r, 512)
    assert r % tr == 0
    return pl.pallas_call(
        _page_prefix_kernel,
        grid=(r // tr,),
        in_specs=[pl.BlockSpec((tr, 1024), lambda i: (i, 0)),
                  pl.BlockSpec((1024, 1024), lambda i: (0, 0)),
                  pl.BlockSpec((1024, 1024), lambda i: (0, 0))],
        out_specs=pl.BlockSpec((tr, 2048), lambda i: (i, 0)),
        out_shape=jax.ShapeDtypeStruct((r, 2048), F32),
        compiler_params=_cparams(("arbitrary",)),
        name="page_prefix",
    )(x, m1, m2)


def _blk_prefix_kernel(wt_ref, ltri_ref, o_ref):
    w = wt_ref[:, 0:1024]
    acc = None
    for p in _split3(wt_ref[:, 1024:2048]):
        d = jnp.dot(ltri_ref[...], p, preferred_element_type=F32)
        acc = d if acc is None else acc + d
    o_ref[...] = -(w + acc) * LOG2E


def _blk_prefix(wt, ltri):
    nb = wt.shape[0]
    return pl.pallas_call(
        _blk_prefix_kernel,
        grid=(1,),
        in_specs=[pl.BlockSpec((nb, 2048), lambda i: (0, 0)), pl.BlockSpec((nb, nb), lambda i: (0, 0))],
        out_specs=pl.BlockSpec((nb, 1024), lambda i: (0, 0)),
        out_shape=jax.ShapeDtypeStruct((nb, 1024), F32),
        compiler_params=_cparams(("arbitrary",)),
        name="blk_prefix",
    )(wt, ltri)


def _softmax_update(s_list, v, m_ref, l_ref, acc_ref, tq):
    tk = s_list[0].shape[1]
    ps, alphas = [], []
    for h in range(N_HEADS):
        rows = slice(h * tq, (h + 1) * tq)
        s = s_list[h]
        m_prev = m_ref[rows, :]
        m_new = jnp.maximum(m_prev, jnp.max(s, axis=1, keepdims=True))
        alpha = jnp.exp2(m_prev - m_new)
        p = jnp.exp2(s - jnp.concatenate([m_new] * (tk // LANES), axis=1))
        l_ref[rows, :] = alpha * l_ref[rows, :] + jnp.sum(p, axis=1, keepdims=True)
        m_ref[rows, :] = m_new
        ps.append(p.astype(BF16))
        alphas.append(alpha)
    pv = jnp.dot(jnp.concatenate(ps, axis=0), v, preferred_element_type=F32)
    acc_ref[...] = jnp.concatenate(alphas, axis=0) * acc_ref[...] + pv


def _init_state(m_ref, l_ref, acc_ref):
    m_ref[...] = jnp.full(m_ref.shape, NEG, F32)
    l_ref[...] = jnp.zeros(l_ref.shape, F32)
    acc_ref[...] = jnp.zeros(acc_ref.shape, F32)


def _float_key(x):
    bits = pltpu.bitcast(x, I32)
    return bits ^ ((bits >> 31) & 0x7FFFFFFF)


def _kth_largest_key(count_ge, k, shape, bits=32):
    zero = jnp.zeros(shape, I32)
    prefix = jnp.where(count_ge(zero) >= k, zero, jnp.full(shape, -(1 << (bits - 1)), I32))

    def bit_body(it, prefix):
        cand = prefix + jnp.left_shift(jnp.int32(1), bits - 2 - it)
        return jnp.where(count_ge(cand) >= k, cand, prefix)

    return lax.fori_loop(0, bits - 1, bit_body, prefix)


def _dsa_prompt_kernel(b31_ref, qi_ref, qa_ref, w_ref, k0_ref, k1_ref, v2_ref, btile_ref, o_ref,
                       key_scr, hi_scr, lo_scr, m_scr, l_scr, acc_scr, *, tq, tk, topk):
    i = pl.program_id(0)
    t0 = i * tq
    nsl = tk // LANES
    qi = qi_ref[...].reshape(N_HEADS * tq, LANES)
    qa = qa_ref[...].reshape(N_HEADS * tq, LANES)
    w = w_ref[...]
    wb = [jnp.broadcast_to(w[:, L_WIDX + h:L_WIDX + h + 1], (tq, tk)) for h in range(N_HEADS)]
    qpos = t0 + lax.broadcasted_iota(I32, (tq, tk), 0)
    lane_k = lax.broadcasted_iota(I32, (tq, tk), 1)
    nc = (t0 + tq + tk - 1) // tk

    def score_body(c, carry):
        k0 = pl.multiple_of(c * tk, tk)
        y = _nt(qi, k1_ref[pl.ds(k0, tk), :])
        acc = wb[0] * jnp.maximum(y[0:tq], 0.0)
        for h in range(1, N_HEADS):
            acc = acc + wb[h] * jnp.maximum(y[h * tq:(h + 1) * tq], 0.0)
        acc = jnp.where(k0 + lane_k <= qpos, acc, -jnp.inf)
        key = _float_key(acc)
        for j in range(nsl):
            kj = key[:, j * LANES:(j + 1) * LANES]
            key_scr[c * nsl + j] = kj
            hi_scr[c * nsl + j] = (kj >> 16).astype(I16)
        return carry

    lax.fori_loop(0, nc, score_body, 0)

    def counter(scr):
        def count_ge(cand):
            cand16 = cand.astype(I16)

            def body(c, cnt):
                for j in range(nsl):
                    cnt = cnt + jnp.where(scr[c * nsl + j] >= cand16, jnp.int16(1), jnp.int16(0))
                return cnt

            cnt = lax.fori_loop(0, nc, body, jnp.zeros((tq, LANES), I16))
            return jnp.sum(cnt.astype(F32), axis=1, keepdims=True)

        return count_ge

    t_hi = _kth_largest_key(counter(hi_scr), float(topk), (tq, LANES), bits=16)

    def low_body(c, carry):
        for j in range(nsl):
            kj = key_scr[c * nsl + j]
            hj = kj >> 16
            lo = (kj & 0xFFFF) - 0x8000
            lo_scr[c * nsl + j] = jnp.where(hj > t_hi, 0x7FFF, jnp.where(hj < t_hi, -0x8000, lo)).astype(I16)
        return carry

    lax.fori_loop(0, nc, low_body, 0)
    t_lo = _kth_largest_key(counter(lo_scr), float(topk), (tq, LANES), bits=16)
    thr = t_hi * 65536 + (t_lo + 0x8000)

    _init_state(m_scr, l_scr, acc_scr)
    far_end = jnp.maximum(t0 - tq, 0)
    nf = (far_end + tk - 1) // tk
    thr_k = jnp.concatenate([thr] * nsl, axis=1)

    def far_body(c, carry):
        k0 = pl.multiple_of(c * tk, tk)
        s = _nt(qa, k0_ref[pl.ds(k0, tk), :])
        keyc = jnp.concatenate([key_scr[c * nsl + j] for j in range(nsl)], axis=1)
        sel = (keyc >= thr_k) & (k0 + lane_k < far_end)
        s_list = [jnp.where(sel, s[h * tq:(h + 1) * tq] + b31_ref[h], NEG) for h in range(N_HEADS)]
        _softmax_update(s_list, v2_ref[pl.ds(k0, tk), :], m_scr, l_scr, acc_scr, tq)
        return carry

    lax.fori_loop(0, nf, far_body, 0)

    ns = pl.multiple_of(far_end, tq)
    sl = far_end // LANES
    s = _nt(qa, k0_ref[pl.ds(ns, 2 * tq), :])
    keyw = jnp.concatenate([key_scr[sl], key_scr[sl + 1]], axis=1)
    kposw = ns + lax.broadcasted_iota(I32, (tq, 2 * tq), 1)
    qposw = t0 + lax.broadcasted_iota(I32, (tq, 2 * tq), 0)
    selw = (keyw >= jnp.concatenate([thr, thr], axis=1)) & (kposw <= qposw)
    first = jnp.where(i == 0, 1, 0)
    s_list = []
    for h in range(N_HEADS):
        bias = jnp.concatenate([btile_ref[first, h], btile_ref[1, h]], axis=1)
        s_list.append(jnp.where(selw, s[h * tq:(h + 1) * tq] + bias, NEG))
    _softmax_update(s_list, v2_ref[pl.ds(ns, 2 * tq), :], m_scr, l_scr, acc_scr, tq)

    lane = lax.broadcasted_iota(I32, (tq, LANES), 1)
    for c in range(N_HEADS // 2):
        r0 = slice(2 * c * tq, (2 * c + 1) * tq)
        r1 = slice((2 * c + 1) * tq, (2 * c + 2) * tq)
        o_ref[:, c * LANES:(c + 1) * LANES] = jnp.where(
            lane < D_HEAD, acc_scr[r0, :] / l_scr[r0, :], acc_scr[r1, :] / l_scr[r1, :])


def _dsa_prompt(b31, qi, qa, st, kvb, btile, topk):
    s = st.shape[0]
    tq, tk = 128, min(512, s)
    assert s % tk == 0 and s >= 2 * tq
    hq = pl.BlockSpec((N_HEADS, tq, 128), lambda i: (0, i, 0))
    col = lambda j: pl.BlockSpec((s, 128), lambda i: (0, j))
    kern = functools.partial(_dsa_prompt_kernel, tq=tq, tk=tk, topk=topk)
    return pl.pallas_call(
        kern,
        grid=(s // tq,),
        in_specs=[pl.BlockSpec(memory_space=pltpu.SMEM),
                  hq, hq,
                  pl.BlockSpec((tq, 128), lambda i: (i, 1)),
                  col(0), col(1), col(6),
                  pl.BlockSpec(btile.shape, lambda i: (0, 0, 0, 0))],
        out_specs=pl.BlockSpec((tq, BRANCH_W), lambda i: (i, 0)),
        out_shape=jax.ShapeDtypeStruct((s, BRANCH_W), F32),
        scratch_shapes=[pltpu.VMEM((s // LANES, tq, LANES), I32),
                        pltpu.VMEM((s // LANES, tq, LANES), I16),
                        pltpu.VMEM((s // LANES, tq, LANES), I16),
                        pltpu.VMEM((N_HEADS * tq, LANES), F32),
                        pltpu.VMEM((N_HEADS * tq, LANES), F32),
                        pltpu.VMEM((N_HEADS * tq, LANES), F32)],
        compiler_params=_cparams(("arbitrary",)),
        name="dsa_prompt",
    )(b31, qi, qa, st, kvb, kvb, kvb, btile)


def _flash_prompt_kernel(q_ref, kv_ref, aux_ref, o_ref, m_scr, l_scr, acc_scr, *, tq, tk, mode):
    i = pl.program_id(0)
    t0 = i * tq
    dq = q_ref.shape[2]
    q = q_ref[...].reshape(N_HEADS * tq, dq)
    nfull = t0 // tk
    _init_state(m_scr, l_scr, acc_scr)

    def kv(k0):
        if mode == "fox":
            return kv_ref[pl.ds(k0, tk), 0:LANES], kv_ref[pl.ds(k0, tk), LANES:2 * LANES]
        k = kv_ref[pl.ds(k0, tk), :]
        return k, k[:, 0:LANES]

    def logits(c, k):
        s = _nt(q, k)
        out = []
        for h in range(N_HEADS):
            sh = s[h * tq:(h + 1) * tq]
            if mode == "fox":
                sh = sh + aux_ref[c, h:h + 1, :]
            out.append(sh)
        return out

    def full_body(c, carry):
        k, v = kv(pl.multiple_of(c * tk, tk))
        _softmax_update(logits(c, k), v, m_scr, l_scr, acc_scr, tq)
        return carry

    lax.fori_loop(0, nfull, full_body, 0)

    k0 = pl.multiple_of(nfull * tk, tk)
    k, v = kv(k0)
    causal = (k0 + lax.broadcasted_iota(I32, (tq, tk), 1)) <= (t0 + lax.broadcasted_iota(I32, (tq, tk), 0))
    s_list = [jnp.where(causal, sh, NEG) for sh in logits(nfull, k)]
    _softmax_update(s_list, v, m_scr, l_scr, acc_scr, tq)

    if mode == "fox":
        lane = lax.broadcasted_iota(I32, (tq, LANES), 1)
        for c in range(N_HEADS // 2):
            r0 = slice(c * tq, (c + 1) * tq)
            r1 = slice((c + 4) * tq, (c + 5) * tq)
            o_ref[:, c * LANES:(c + 1) * LANES] = jnp.where(
                lane < D_HEAD, acc_scr[r0, :] / l_scr[r0, :], acc_scr[r1, :] / l_scr[r1, :])
    else:
        out = None
        for h in range(N_HEADS):
            rows = slice(h * tq, (h + 1) * tq)
            olat = (acc_scr[rows, :] / l_scr[rows, :]).astype(BF16)
            d = jnp.dot(olat, aux_ref[h], preferred_element_type=F32)
            out = d if out is None else out + d
        o_ref[...] = out


def _flash_prompt(q, kvb, aux, mode):
    s = kvb.shape[0]
    tq, tk = 128, min(512, s)
    dq = q.shape[2]
    kern = functools.partial(_flash_prompt_kernel, tq=tq, tk=tk, mode=mode)
    return pl.pallas_call(
        kern,
        grid=(s // tq,),
        in_specs=[pl.BlockSpec((N_HEADS, tq, dq), lambda i: (0, i, 0)),
                  pl.BlockSpec((s, 256), lambda i: (0, 1 if mode == "fox" else 2)),
                  pl.BlockSpec(aux.shape, lambda i: (0, 0, 0))],
        out_specs=pl.BlockSpec((tq, BRANCH_W), lambda i: (i, 0)),
        out_shape=jax.ShapeDtypeStruct((s, BRANCH_W), F32),
        scratch_shapes=[pltpu.VMEM((N_HEADS * tq, LANES), F32),
                        pltpu.VMEM((N_HEADS * tq, LANES), F32),
                        pltpu.VMEM((N_HEADS * tq, LANES), F32)],
        compiler_params=_cparams(("arbitrary",)),
        name=mode + "_prompt",
    )(q, kvb, aux)


def _merge_kernel(oa_ref, ob_ref, oc_ref, z_ref, g_ref, x_ref, wbr_ref, wout_ref, lng_ref, lnb_ref, o_ref,
                  *, alpha):
    mixed = None
    for n, o in enumerate((oa_ref, ob_ref, oc_ref)):
        zn = z_ref[:, n * BRANCH_W:(n + 1) * BRANCH_W]
        u = o[...] * (zn * jax.nn.sigmoid(zn))
        y = jnp.dot(u.astype(BF16), wbr_ref[n], preferred_element_type=F32)
        t = jax.nn.sigmoid(g_ref[:, n * D_MODEL:(n + 1) * D_MODEL]) * y
        mixed = t if mixed is None else mixed + t
    out = jnp.dot(mixed.astype(BF16), wout_ref[...], preferred_element_type=F32)
    hres = alpha * x_ref[...] + out
    mu = jnp.mean(hres, axis=-1, keepdims=True)
    d = hres - mu
    var = jnp.mean(d * d, axis=-1, keepdims=True)
    o_ref[...] = d * lax.rsqrt(var + LN_EPS) * lng_ref[...] + lnb_ref[...]


def _merge(oa, ob, oc, y, x, wbr, wout, lng, lnb, alpha):
    m = x.shape[0]
    tm = min(m, 256)
    row = lambda w: pl.BlockSpec((tm, w), lambda i: (i, 0))
    full = lambda a: pl.BlockSpec(a.shape, lambda i: (0,) * a.ndim)
    return pl.pallas_call(
        functools.partial(_merge_kernel, alpha=alpha),
        grid=(m // tm,),
        in_specs=[row(BRANCH_W), row(BRANCH_W), row(BRANCH_W),
                  pl.BlockSpec((tm, 1536), lambda i: (i, OFF_Z // 1536)),
                  pl.BlockSpec((tm, 3072), lambda i: (i, OFF_G // 3072)),
                  row(D_MODEL), full(wbr), full(wout), full(lng), full(lnb)],
        out_specs=row(D_MODEL),
        out_shape=jax.ShapeDtypeStruct((m, D_MODEL), F32),
        compiler_params=_cparams(("arbitrary",)),
        name="merge",
    )(oa, ob, oc, y, y, x, wbr, wout, lng, lnb)


def _np_consts():
    sela = np.zeros((512, 1024), np.float32)
    selb = np.zeros((512, 1024), np.float32)
    for h in range(N_HEADS):
        for d in range(D_HEAD):
            sela[h * 64 + d, h * 128 + d] = 1.0
            selb[h * 64 + d, h * 128 + (0 if h < 4 else 64) + d] = 1.0
    psel = np.zeros((256, 1024), np.float32)
    for h in range(N_HEADS):
        for j in range(16):
            psel[h * 16 + j, h * 128 + j] = 1.0
            psel[128 + h * 16 + j, h * 128 + 16 + j] = 1.0
    i = np.arange(1024)
    r, h = i // 8, i % 8
    hp, rp = i // 128, i % 128
    m2 = (h[:, None] == hp[None, :])
    m1 = m2 & (r[:, None] <= rp[None, :])
    return sela, selb, psel, m1.astype(np.float32), m2.astype(np.float32)


def _t5_bucket(dist):
    max_exact = N_BUCKETS // 2
    d = jnp.maximum(dist, 1).astype(F32)
    large = max_exact + (jnp.log(d / max_exact) / np.log(MAX_DISTANCE / max_exact)
                         * (N_BUCKETS - max_exact)).astype(I32)
    large = jnp.minimum(large, N_BUCKETS - 1)
    return jnp.where(dist < max_exact, dist, large)


def _rope_tables(pos):
    half = D_ROPE // 2
    freqs = ROPE_THETA ** (-jnp.arange(half, dtype=F32) / half)
    ang = pos.astype(F32)[:, None] * freqs
    cos, sin = jnp.cos(ang), jnp.sin(ang)
    m = pos.shape[0]
    cosq, sinq = jnp.tile(cos, (1, N_HEADS)), jnp.tile(sin, (1, N_HEADS))
    cosk = jnp.ones((m, LANES), F32).at[:, L_KROPE:L_KROPE + 16].set(cos).at[:, L_KROPE + 16:L_KROPE + 32].set(cos)
    sink = jnp.zeros((m, LANES), F32).at[:, L_KROPE:L_KROPE + 16].set(-sin).at[:, L_KROPE + 16:L_KROPE + 32].set(sin)
    return cosq, sinq, cosk, sink


def _prep_layer(l, w_in, b_forget, w_c_uq, g_c_q, g_c_kv, w_c_ukv, w_branch, w_out, ln_g, ln_b):
    sela, selb, psel, _, _ = _np_consts()
    src, scale = _col_perm()
    wp = (jnp.take(w_in[l], jnp.asarray(src), axis=1) * jnp.asarray(scale)).astype(BF16)
    bf_row = jnp.zeros((1, LANES), F32).at[0, L_BF:L_BF + N_HEADS].set(b_forget[l])
    wq = w_c_uq[l]
    wq2 = jnp.concatenate([wq[:, :, :D_NOPE].reshape(D_CQ, 512),
                           wq[:, :, D_NOPE:D_NOPE + 16].reshape(D_CQ, 128),
                           wq[:, :, D_NOPE + 16:].reshape(D_CQ, 128)], axis=1).astype(BF16)
    w_uk, w_uv = w_c_ukv[l][..., :D_NOPE], w_c_ukv[l][..., D_NOPE:]
    eye = jnp.eye(N_HEADS, dtype=F32)
    wukbd = jnp.einsum('chn,hg->hngc', w_uk, eye).reshape(512, 1024).astype(BF16)
    wuvpad = jnp.einsum('chv,hg->hcgv', w_uv, eye).reshape(N_HEADS, D_LATENT, 512).astype(BF16)
    wbr = w_branch[l].at[1].set(w_branch[l][1][jnp.asarray(_fox_head_perm())]).astype(BF16)
    return dict(
        wp=wp, wuvpad=wuvpad, wbr=wbr, wout=w_out[l].astype(BF16), lng=ln_g[l][None], lnb=ln_b[l][None],
        post=(bf_row, g_c_q[l][None], g_c_kv[l][None], jnp.asarray(sela, BF16), jnp.asarray(selb, BF16),
              wq2, wukbd, jnp.asarray(psel, BF16)))


def _bias_tables(t5_table):
    r = jnp.arange(PAGE)[:, None]
    c = jnp.arange(PAGE)[None, :]
    dist = jnp.stack([PAGE + r - c, jnp.maximum(r - c, 0)])
    btile = jnp.transpose(t5_table[_t5_bucket(dist)], (0, 3, 1, 2))
    bfar = t5_table[_t5_bucket(jnp.asarray(2 * PAGE))]
    return btile.astype(F32) * LOG2E, bfar.astype(F32) * LOG2E


def _prompt_layer(xp, prep, tabs, btile, bfar, topk, alpha, m1, m2):
    s = xp.shape[0]
    y = _project(xp, prep['wp'])
    qa, qi, qb, qc, st, kvb = _post(y, tabs, prep['post'])
    nb = s // PAGE
    wt = _page_prefix(st[:, 128 + L_BF:128 + L_BF + N_HEADS].reshape(nb, 1024), m1, m2)
    ltri = jnp.asarray(np.tril(np.ones((nb, nb), np.float32), -1), BF16)
    nf = _blk_prefix(wt, ltri)
    tk = min(512, s)
    nf = jnp.transpose(nf.reshape(nb, N_HEADS, PAGE), (1, 0, 2)).reshape(N_HEADS, s // tk, tk)
    nf = jnp.transpose(nf, (1, 0, 2))
    o_a = _dsa_prompt(bfar, qi, qa, st, kvb, btile, topk)
    o_b = _flash_prompt(qb, kvb, nf, "fox")
    o_c = _flash_prompt(qc, kvb, prep['wuvpad'], "mla")
    xp = _merge(o_a, o_b, o_c, y, xp, prep['wbr'], prep['wout'], prep['lng'], prep['lnb'], alpha)
    return xp, st


def _state_rows(st, lead):
    f = lambda a, *tail: a.reshape(lead + tail)
    return (f(st[:, 0:64], 64), f(st[:, 64:128], 64), f(st[:, 128:192], 64),
            f(st[:, 256:384], KV_B, D_HEAD), f(st[:, 384:512], KV_B, D_HEAD),
            f(st[:, 128 + L_BF:128 + L_BF + N_HEADS], N_HEADS),
            f(st[:, 512:640], D_LATENT), f(st[:, 128 + L_KROPE:128 + L_KROPE + D_ROPE], D_ROPE))


def _page_pipeline(pt_ref, caches, bufs, sems, layer, pps):
    b, c = pl.program_id(0), pl.program_id(1)
    nb, nch = pl.num_programs(0), pl.num_programs(1)
    n = b * nch + c
    slot = lax.rem(n, 2)

    def copies(bb, cc, sl):
        out = []
        for cache, buf, sem in zip(caches, bufs, sems):
            for p in range(pps):
                page = pt_ref[bb, cc * pps + p]
                out.append(pltpu.make_async_copy(cache.at[layer, page], buf.at[sl, p], sem.at[sl]))
        return out

    @pl.when(n == 0)
    def _():
        for cp in copies(b, c, slot):
            cp.start()

    @pl.when(n + 1 < nb * nch)
    def _():
        wrap = c + 1 == nch
        for cp in copies(jnp.where(wrap, b + 1, b), jnp.where(wrap, 0, c + 1), 1 - slot):
            cp.start()

    for cp in copies(b, c, slot):
        cp.wait()
    return slot


def _page_scratch(page_shapes, pps):
    return ([pltpu.VMEM((2, pps) + tuple(ps), F32) for ps in page_shapes]
            + [pltpu.SemaphoreType.DMA((2,)) for _ in page_shapes])


_ANY = pl.BlockSpec(memory_space=pl.ANY)


def _pages_t(cache):
    nd = cache.ndim
    return jnp.transpose(cache, (0, 1) + tuple(range(3, nd)) + (2,))


def _seq_spec(*shape):
    nd = len(shape)
    return pl.BlockSpec((None,) + shape, lambda b, c, pt: (b,) + (0,) * nd)


def _tile_lanes(x, n):
    return jnp.concatenate([x] * n, axis=1) if n > 1 else x


def _dec_update(s, pv_fn, m_ref, l_ref, acc_ref):
    dv = acc_ref.shape[1]
    m_prev = m_ref[...]
    m_new = jnp.maximum(m_prev, jnp.max(s, axis=1, keepdims=True))
    alpha = jnp.exp2(m_prev - m_new)
    p = jnp.exp2(s - _tile_lanes(m_new, s.shape[1] // LANES))
    l_ref[...] = alpha * l_ref[...] + jnp.sum(p, axis=1, keepdims=True)
    m_ref[...] = m_new
    acc_ref[...] = alpha[:, 0:dv] * acc_ref[...] + pv_fn(p.astype(BF16))


def _dec_self(s_self, v_self, m_ref, l_ref, acc_ref):
    dv = acc_ref.shape[1]
    m_prev = m_ref[...]
    m_new = jnp.maximum(m_prev, s_self)
    alpha = jnp.exp2(m_prev - m_new)
    p = jnp.exp2(s_self - m_new)
    l_ref[...] = alpha * l_ref[...] + p
    m_ref[...] = m_new
    acc_ref[...] = alpha[:, 0:dv] * acc_ref[...] + p[:, 0:dv] * v_self


def _idx_sample_kernel(pt_ref, q_ref, w_ref, g1_ref, kidx_hbm, sc_ref, self_ref, kbuf, ksem, *, pps, layer):
    slot = _page_pipeline(pt_ref, [kidx_hbm], [kbuf], [ksem], layer, pps)
    q = q_ref[...]
    w = w_ref[...]
    kt = jnp.concatenate([kbuf[slot, p] for p in range(pps)], axis=1).astype(BF16)
    y = jnp.dot(q[:, 0:D_HEAD], kt, preferred_element_type=F32)
    sc_ref[...] = jnp.sum(jnp.maximum(y, 0.0) * _tile_lanes(w, pps), axis=0, keepdims=True)

    @pl.when(pl.program_id(1) == 0)
    def _():
        ys = jnp.sum(q.astype(F32) * g1_ref[...], axis=1, keepdims=True)
        val = jnp.sum(jnp.maximum(ys, 0.0) * w[:, 0:1], axis=0, keepdims=True)
        lane = lax.broadcasted_iota(I32, (1, LANES), 1)
        self_ref[...] = jnp.where(lane == 0, val, -jnp.inf)


def _idx_sample(page_table, layer, qi_t, w8, g1new, cache_kidx, pps):
    b, npages = page_table.shape
    nch = npages // pps
    grid_spec = pltpu.PrefetchScalarGridSpec(
        num_scalar_prefetch=1, grid=(b, nch),
        in_specs=[_seq_spec(N_HEADS, LANES), _seq_spec(N_HEADS, LANES), _seq_spec(1, LANES), _ANY],
        out_specs=[pl.BlockSpec((None, 1, PAGE * pps), lambda b, c, pt: (b, 0, c)), _seq_spec(1, LANES)],
        scratch_shapes=_page_scratch([(D_HEAD, PAGE)], pps))
    return pl.pallas_call(
        functools.partial(_idx_sample_kernel, pps=pps, layer=layer), grid_spec=grid_spec,
        out_shape=[jax.ShapeDtypeStruct((b, 1, npages * PAGE), F32), jax.ShapeDtypeStruct((b, 1, LANES), F32)],
        compiler_params=_cparams(("arbitrary", "arbitrary")), name="idx_sample",
    )(page_table, qi_t, w8, g1new, cache_kidx)


def _thresh_kernel(sc_ref, o_ref, key_scr, *, topk):
    rows = sc_ref.shape[0]
    nsl = sc_ref.shape[1] // LANES
    for j in range(nsl):
        key_scr[j] = _float_key(sc_ref[:, j * LANES:(j + 1) * LANES])

    def count_ge(cand):
        cnt = lax.fori_loop(0, nsl, lambda j, cnt: cnt + (key_scr[j] >= cand).astype(I32),
                            jnp.zeros((rows, LANES), I32))
        return jnp.sum(cnt.astype(F32), axis=1, keepdims=True)

    thr = _kth_largest_key(count_ge, float(topk), (rows, LANES))
    bits = thr ^ ((thr >> 31) & 0x7FFFFFFF)
    o_ref[...] = jnp.where(thr == INT_MIN, -jnp.inf, pltpu.bitcast(bits, F32))


def _thresh(sc, topk):
    rows, n = sc.shape
    return pl.pallas_call(
        functools.partial(_thresh_kernel, topk=topk), grid=(1,),
        in_specs=[pl.BlockSpec((rows, n), lambda i: (0, 0))],
        out_specs=pl.BlockSpec((rows, LANES), lambda i: (0, 0)),
        out_shape=jax.ShapeDtypeStruct((rows, LANES), F32),
        scratch_shapes=[pltpu.VMEM((n // LANES, rows, LANES), I32)],
        compiler_params=_cparams(("arbitrary",)), name="thresh_sample",
    )(sc)


def _dsa_sample_kernel(pt_ref, q_ref, sc_ref, self_ref, thr_ref, g0_ref, bfar_ref, blast_ref, bself_ref,
                       k_hbm, v_hbm, o_ref, m_scr, l_scr, acc_scr, kbuf, vbuf, ksem, vsem, *, pps, layer):
    slot = _page_pipeline(pt_ref, [k_hbm, v_hbm], [kbuf, vbuf], [ksem, vsem], layer, pps)
    c = pl.program_id(1)
    last = c == pl.num_programs(1) - 1

    @pl.when(c == 0)
    def _():
        _init_state(m_scr, l_scr, acc_scr)

    q = q_ref[...]
    kt = jnp.concatenate([kbuf[slot, p] for p in range(pps)], axis=1).astype(BF16)
    vt = jnp.concatenate([vbuf[slot, p] for p in range(pps)], axis=1).astype(BF16)
    thr = thr_ref[...]
    sel = sc_ref[...] >= _tile_lanes(thr, pps)
    bfar = bfar_ref[...]
    bias = jnp.concatenate([bfar] * (pps - 1) + [jnp.where(last, blast_ref[...], bfar)], axis=1)
    s = jnp.where(sel, jnp.dot(q[:, 0:D_HEAD], kt, preferred_element_type=F32) + bias, NEG)
    _dec_update(s, lambda p: _nt(p, vt), m_scr, l_scr, acc_scr)

    @pl.when(last)
    def _():
        g0 = g0_ref[...]
        s_self = jnp.sum(q.astype(F32) * g0, axis=1, keepdims=True) + bself_ref[...]
        s_self = jnp.where(self_ref[:, 0:1] >= thr[:, 0:1], s_self, NEG)
        _dec_self(s_self, g0[:, D_HEAD:2 * D_HEAD], m_scr, l_scr, acc_scr)
        o_ref[...] = acc_scr[...] / l_scr[:, 0:D_HEAD]


def _dsa_sample(page_table, layer, qa_t, sc, selfsc, thr, g0new, bfar8, blast, bself8, cache_k, cache_v, pps):
    b, npages = page_table.shape
    full = lambda a: pl.BlockSpec(a.shape, lambda b, c, pt: (0,) * a.ndim)
    grid_spec = pltpu.PrefetchScalarGridSpec(
        num_scalar_prefetch=1, grid=(b, npages // pps),
        in_specs=[_seq_spec(N_HEADS, LANES), pl.BlockSpec((None, 1, PAGE * pps), lambda b, c, pt: (b, 0, c)),
                  _seq_spec(1, LANES), _seq_spec(1, LANES), _seq_spec(1, LANES), full(bfar8), full(blast), full(bself8),
                  _ANY, _ANY],
        out_specs=_seq_spec(N_HEADS, D_HEAD),
        scratch_shapes=[pltpu.VMEM((N_HEADS, LANES), F32), pltpu.VMEM((N_HEADS, LANES), F32),
                        pltpu.VMEM((N_HEADS, D_HEAD), F32)] + _page_scratch([(D_HEAD, PAGE)] * 2, pps))
    return pl.pallas_call(
        functools.partial(_dsa_sample_kernel, pps=pps, layer=layer), grid_spec=grid_spec,
        out_shape=jax.ShapeDtypeStruct((b, N_HEADS, D_HEAD), F32),
        compiler_params=_cparams(("arbitrary", "arbitrary")), name="dsa_sample",
    )(page_table, qa_t, sc, selfsc, thr, g0new, bfar8, blast, bself8, cache_k, cache_v)


def _fox_sample_kernel(pt_ref, q_ref, st_ref, lf_ref, utri_ref, ones_ref, k_hbm, v_hbm, f_hbm,
                       o_ref, m_scr, l_scr, acc_scr, carry_scr, kbuf, vbuf, fbuf, ksem, vsem, fsem, *, pps, layer):
    slot = _page_pipeline(pt_ref, [k_hbm, v_hbm, f_hbm], [kbuf, vbuf, fbuf], [ksem, vsem, fsem], layer, pps)
    kp = [kbuf.at[slot, p] for p in range(pps)]
    vp = [vbuf.at[slot, p] for p in range(pps)]
    fp = [fbuf.at[slot, p] for p in range(pps)]
    c = pl.program_id(1)

    @pl.when(c == 0)
    def _():
        _init_state(m_scr, l_scr, acc_scr)
        carry_scr[...] = jnp.zeros(carry_scr.shape, F32)

    q = q_ref[...]
    q0, q1 = q[:, 0:D_HEAD], q[:, D_HEAD:2 * D_HEAD]
    kv = lambda pages, g: jnp.concatenate([p[g] for p in pages], axis=1).astype(BF16)
    s = (jnp.dot(q0, kv(kp, 0), preferred_element_type=F32)
         + jnp.dot(q1, kv(kp, 1), preferred_element_type=F32))
    parts = _split3(jnp.concatenate([p[...] for p in fp], axis=0))
    wsum = tsum = None
    for part in parts:
        dw = jnp.dot(part, utri_ref[...], preferred_element_type=F32)
        dt = jnp.dot(part, ones_ref[...], preferred_element_type=F32)
        wsum = dw if wsum is None else wsum + dw
        tsum = dt if tsum is None else tsum + dt
    po = carry_scr[...]
    biases = []
    for p in range(pps):
        biases.append(wsum[p * N_HEADS:(p + 1) * N_HEADS, :] + po)
        po = po + tsum[p * N_HEADS:(p + 1) * N_HEADS, :]
    carry_scr[...] = po
    s = s - jnp.concatenate(biases, axis=1) * LOG2E
    v0, v1 = kv(vp, 0), kv(vp, 1)
    row = lax.broadcasted_iota(I32, (N_HEADS, D_HEAD), 0)
    lower = row < N_HEADS // KV_B
    _dec_update(s, lambda p: jnp.where(lower, _nt(p, v0), _nt(p, v1)), m_scr, l_scr, acc_scr)

    @pl.when(c == pl.num_programs(1) - 1)
    def _():
        k_new, v_new = st_ref[:, 256:384], st_ref[:, 384:512]
        s_self = jnp.sum(q.astype(F32) * k_new, axis=1, keepdims=True) - (po + lf_ref[...]) * LOG2E
        v_self = jnp.where(lower, jnp.broadcast_to(v_new[:, 0:D_HEAD], (N_HEADS, D_HEAD)),
                           jnp.broadcast_to(v_new[:, D_HEAD:2 * D_HEAD], (N_HEADS, D_HEAD)))
        _dec_self(s_self, v_self, m_scr, l_scr, acc_scr)
        o_ref[...] = acc_scr[...] / l_scr[:, 0:D_HEAD]


def _fox_sample(page_table, layer, qb_t, st3, lf8, cache_k, cache_v, cache_f, pps):
    b, npages = page_table.shape
    utri = jnp.asarray(np.triu(np.ones((PAGE, PAGE), np.float32)), BF16)
    ones = jnp.ones((PAGE, PAGE), BF16)
    full = lambda a: pl.BlockSpec(a.shape, lambda b, c, pt: (0,) * a.ndim)
    grid_spec = pltpu.PrefetchScalarGridSpec(
        num_scalar_prefetch=1, grid=(b, npages // pps),
        in_specs=[_seq_spec(N_HEADS, LANES), _seq_spec(1, 640), _seq_spec(N_HEADS, LANES), full(utri), full(ones),
                  _ANY, _ANY, _ANY],
        out_specs=_seq_spec(N_HEADS, D_HEAD),
        scratch_shapes=[pltpu.VMEM((N_HEADS, LANES), F32), pltpu.VMEM((N_HEADS, LANES), F32),
                        pltpu.VMEM((N_HEADS, D_HEAD), F32), pltpu.VMEM((N_HEADS, LANES), F32)]
        + _page_scratch([(KV_B, D_HEAD, PAGE), (KV_B, D_HEAD, PAGE), (N_HEADS, PAGE)], pps))
    return pl.pallas_call(
        functools.partial(_fox_sample_kernel, pps=pps, layer=layer), grid_spec=grid_spec,
        out_shape=jax.ShapeDtypeStruct((b, N_HEADS, D_HEAD), F32),
        compiler_params=_cparams(("arbitrary", "arbitrary")), name="fox_sample",
    )(page_table, qb_t, st3, lf8, utri, ones, cache_k, cache_v, cache_f)


def _mla_sample_kernel(pt_ref, q_ref, kn_ref, c_hbm, r_hbm, o_ref, m_scr, l_scr, acc_scr,
                       cbuf, rbuf, csem, rsem, *, pps, layer):
    slot = _page_pipeline(pt_ref, [c_hbm, r_hbm], [cbuf, rbuf], [csem, rsem], layer, pps)
    c = pl.program_id(1)

    @pl.when(c == 0)
    def _():
        _init_state(m_scr, l_scr, acc_scr)

    q = q_ref[...]
    ckv = jnp.concatenate([cbuf[slot, p] for p in range(pps)], axis=0).astype(BF16)
    krt = jnp.concatenate([rbuf[slot, p] for p in range(pps)], axis=1).astype(BF16)
    s = _nt(q[:, 0:D_LATENT], ckv) + jnp.dot(q[:, D_LATENT:D_LATENT + D_ROPE], krt, preferred_element_type=F32)
    _dec_update(s, lambda p: jnp.dot(p, ckv, preferred_element_type=F32), m_scr, l_scr, acc_scr)

    @pl.when(c == pl.num_programs(1) - 1)
    def _():
        kn = kn_ref[...].astype(F32)
        s_self = jnp.sum(q.astype(F32) * kn, axis=1, keepdims=True)
        _dec_self(s_self, kn[:, 0:D_LATENT], m_scr, l_scr, acc_scr)
        o_ref[...] = acc_scr[...] / l_scr[...]


def _mla_sample(page_table, layer, qc_t, kvb3, cache_lat, cache_kr, pps):
    b, npages = page_table.shape
    grid_spec = pltpu.PrefetchScalarGridSpec(
        num_scalar_prefetch=1, grid=(b, npages // pps),
        in_specs=[_seq_spec(N_HEADS, 2 * LANES), pl.BlockSpec((None, 1, 2 * LANES), lambda b, c, pt: (b, 0, 2)),
                  _ANY, _ANY],
        out_specs=_seq_spec(N_HEADS, D_LATENT),
        scratch_shapes=[pltpu.VMEM((N_HEADS, LANES), F32), pltpu.VMEM((N_HEADS, LANES), F32),
                        pltpu.VMEM((N_HEADS, D_LATENT), F32)]
        + _page_scratch([(PAGE, D_LATENT), (D_ROPE, PAGE)], pps))
    return pl.pallas_call(
        functools.partial(_mla_sample_kernel, pps=pps, layer=layer), grid_spec=grid_spec,
        out_shape=jax.ShapeDtypeStruct((b, N_HEADS, D_LATENT), F32),
        compiler_params=_cparams(("arbitrary", "arbitrary")), name="mla_sample",
    )(page_table, qc_t, kvb3, cache_lat, cache_kr)


def _uv_kernel(o_ref, w_ref, out_ref):
    out = None
    for h in range(N_HEADS):
        d = jnp.dot(o_ref[h].astype(BF16), w_ref[h], preferred_element_type=F32)
        out = d if out is None else out + d
    out_ref[...] = out


def _uv_proj(olat_t, wuvpad):
    b = olat_t.shape[1]
    return pl.pallas_call(
        _uv_kernel, grid=(1,),
        in_specs=[pl.BlockSpec(olat_t.shape, lambda i: (0, 0, 0)), pl.BlockSpec(wuvpad.shape, lambda i: (0, 0, 0))],
        out_specs=pl.BlockSpec((b, BRANCH_W), lambda i: (0, 0)),
        out_shape=jax.ShapeDtypeStruct((b, BRANCH_W), F32),
        compiler_params=_cparams(("arbitrary",)), name="uv_proj",
    )(olat_t, wuvpad)


def _sample_layer(xs, layer, prep, tabs, caches, page_table, bias_s, topk, alpha, pps):
    ca_k, ca_v, ca_kidx, cb_k, cb_v, cb_f, cc_lat, cc_kr = caches
    bfar8, blast, bself8 = bias_s
    b = xs.shape[0]
    y = _project(xs, prep['wp'])
    qa, qi, qb, qc, st, kvb = _post(y, tabs, prep['post'])
    tr = lambda q: jnp.transpose(q, (1, 0, 2))
    g1 = st[:, 128:256]
    bc8 = lambda a: jnp.broadcast_to(a[:, :, None], (b, N_HEADS, LANES))
    w8 = bc8(g1[:, L_WIDX:L_WIDX + N_HEADS])
    lf8 = bc8(g1[:, L_BF:L_BF + N_HEADS])
    st3 = st[:, None, :]
    sc, selfsc = _idx_sample(page_table, layer, tr(qi), w8, g1[:, None, :], ca_kidx, pps)
    thr = _thresh(jnp.concatenate([sc[:, 0, :], selfsc[:, 0, :]], axis=1), topk)
    o_a = _dsa_sample(page_table, layer, tr(qa), sc, selfsc, thr[:, None, :], st3[:, :, 0:128], bfar8, blast, bself8,
                      ca_k, ca_v, pps)
    o_b = _fox_sample(page_table, layer, tr(qb), st3, lf8, cb_k, cb_v, cb_f, pps)
    olat = _mla_sample(page_table, layer, tr(qc), kvb[:, None, :], cc_lat, cc_kr, pps)
    o_c = _uv_proj(tr(olat), prep['wuvpad'])
    o_b = o_b.reshape(b, BRANCH_W)[:, jnp.asarray(_fox_head_perm())]
    xs = _merge(o_a.reshape(b, BRANCH_W), o_b, o_c, y, xs, prep['wbr'], prep['wout'], prep['lng'], prep['lnb'], alpha)
    return xs, st


def kernel(x_prompt, x_sample, cache_a_k, cache_a_v, cache_a_kidx, cache_b_k, cache_b_v, cache_b_logf,
           cache_c_latent, cache_c_krope, page_table, t5_table, w_in, b_forget, w_c_uq, g_c_q, g_c_kv,
           w_c_ukv, w_branch, w_out, ln_g, ln_b):
    depth = w_in.shape[0]
    bp, s, _ = x_prompt.shape
    bs, ds, _ = x_sample.shape
    npages = page_table.shape[1]
    past = npages * PAGE
    n_pool = cache_a_k.shape[1]
    assert bp == 1 and ds == 1 and cache_a_k.shape[2] == PAGE
    alpha = float((2 * depth) ** 0.25)
    pps = 16 if npages % 16 == 0 and npages >= 32 else npages // 2
    _, _, _, m1, m2 = _np_consts()
    m1, m2 = jnp.asarray(m1, BF16), jnp.asarray(m2, BF16)
    btile, bfar = _bias_tables(t5_table)
    bc = lambda a: jnp.broadcast_to(a[:, None], (N_HEADS, LANES)).astype(F32)
    dist_last = PAGE - jnp.arange(PAGE)
    bias_s = (bc(bfar), jnp.transpose(t5_table[_t5_bucket(dist_last)]).astype(F32) * LOG2E,
              bc(t5_table[_t5_bucket(jnp.asarray(0))]) * LOG2E)
    tabs_p = _rope_tables(jnp.arange(s, dtype=I32))
    tabs_s = _rope_tables(jnp.full((bs,), past, I32))
    caches = (_pages_t(cache_a_k), _pages_t(cache_a_v), _pages_t(cache_a_kidx), _pages_t(cache_b_k),
              _pages_t(cache_b_v), _pages_t(cache_b_logf), cache_c_latent, _pages_t(cache_c_krope))
    xp = x_prompt.reshape(s, D_MODEL)
    xs = x_sample.reshape(bs, D_MODEL)
    rows_p, rows_s = [], []
    for l in range(depth):
        prep = _prep_layer(l, w_in, b_forget, w_c_uq, g_c_q, g_c_kv, w_c_ukv, w_branch, w_out, ln_g, ln_b)
        xp, st_p = _prompt_layer(xp, prep, tabs_p, btile, bfar, min(TOPK_MAX, s // 4), alpha, m1, m2)
        rows_p.append(_state_rows(st_p, (bp, s)))
        xs, st_s = _sample_layer(xs, l, prep, tabs_s, caches, page_table, bias_s,
                                 min(TOPK_MAX, (past + 1) // 4), alpha, pps)
        rows_s.append(_state_rows(st_s, (bs, 1)))
    new_p = [jnp.stack([r[i] for r in rows_p], axis=0) for i in range(8)]
    new_s = [jnp.stack([r[i] for r in rows_s], axis=0) for i in range(8)]
    return (xp.reshape(bp, s, D_MODEL), xs.reshape(bs, 1, D_MODEL), *new_p, *new_s)
```

```python
import functools

import numpy as np
import jax
import jax.numpy as jnp
from jax import lax
from jax.experimental import pallas as pl
from jax.experimental.pallas import tpu as pltpu

F32 = jnp.float32
BF16 = jnp.bfloat16
I32 = jnp.int32

N_HEADS = 8
D_HEAD = 64
KV_B = 2
D_NOPE = 64
D_ROPE = 32
D_CQ = 256
D_LATENT = 128
TOPK_MAX = 256
ROPE_THETA = 10000.0
N_BUCKETS = 32
MAX_DISTANCE = 128
BRANCH_W = 512
D_MODEL = 1024
PAGE = 128
LN_EPS = 1e-5
RMS_EPS = 1e-6

LANES = 128
NEG = -1e30
LOG2E = 1.4426950408889634
INT_MIN = int(np.iinfo(np.int32).min)
VMEM_LIMIT = 56 * 1024 * 1024
FLASH_TK = 512
PV_GROUPS = 1

SRC_SPLITS = (
    ('a_q', 512), ('a_k', 64), ('a_v', 64), ('a_qidx', 512), ('a_kidx', 64), ('a_widx', 8), ('a_z', 512),
    ('b_q', 512), ('b_k', 128), ('b_v', 128), ('b_f', 8), ('b_z', 512),
    ('c_q', 256), ('c_kv', 128), ('c_krope', 32), ('c_z', 512), ('gates', 3072))
NP = 7168
OFF_Q = 0
OFF_Z = 1536
OFF_G = 3072
OFF_S = 6144
L_KROPE, L_BF, L_WIDX = 64, 96, 104


def _fox_head_perm():
    j = np.arange(BRANCH_W)
    return ((j // 128) + 4 * ((j % 128) // 64)) * 64 + (j % 64)


def _col_perm():
    src_off, o = {}, 0
    for name, n in SRC_SPLITS:
        src_off[name] = o
        o += n
    src = -np.ones((NP,), np.int64)
    scale = np.ones((NP,), np.float32)

    def put(dst, name, n, perm=None):
        idx = np.arange(n) if perm is None else perm
        src[dst:dst + n] = src_off[name] + idx

    put(0, 'a_q', 512)
    scale[0:512] = D_HEAD ** -0.5
    put(512, 'a_qidx', 512)
    put(1024, 'b_q', 512)
    scale[1024:1536] = D_HEAD ** -0.5
    put(OFF_Z, 'a_z', 512)
    put(OFF_Z + 512, 'b_z', 512, _fox_head_perm())
    put(OFF_Z + 1024, 'c_z', 512)
    put(OFF_G, 'gates', 3072)
    put(OFF_S, 'c_q', 256)
    g0 = OFF_S + 256
    put(g0, 'a_k', 64)
    put(g0 + 64, 'a_v', 64)
    g1 = g0 + 128
    put(g1, 'a_kidx', 64)
    put(g1 + L_KROPE, 'c_krope', 32)
    put(g1 + L_BF, 'b_f', 8)
    put(g1 + L_WIDX, 'a_widx', 8)
    put(g1 + 128, 'b_k', 128)
    put(g1 + 256, 'b_v', 128)
    put(g1 + 384, 'c_kv', 128)
    scale[src < 0] = 0.0
    return np.maximum(src, 0), scale


def _split3(x):
    hi = x.astype(BF16)
    r = x - hi.astype(F32)
    mid = r.astype(BF16)
    lo = (r - mid.astype(F32)).astype(BF16)
    return lo, mid, hi


def _nt(a, b):
    return lax.dot_general(a, b, (((1,), (1,)), ((), ())), preferred_element_type=F32)


def _cparams(sem):
    return pltpu.CompilerParams(dimension_semantics=sem, vmem_limit_bytes=VMEM_LIMIT)


def _mm_kernel(x_ref, w_ref, o_ref, xb_ref):
    @pl.when(pl.program_id(1) == 0)
    def _():
        xb_ref[...] = x_ref[...].astype(BF16)

    o_ref[...] = jnp.dot(xb_ref[...], w_ref[...], preferred_element_type=F32)


def _project(x, wp):
    m, k = x.shape
    n = wp.shape[1]
    tm = min(m, 1024)
    tn = 1024
    return pl.pallas_call(
        _mm_kernel,
        grid=(m // tm, n // tn),
        in_specs=[pl.BlockSpec((tm, k), lambda i, j: (i, 0)),
                  pl.BlockSpec((k, tn), lambda i, j: (0, j))],
        out_specs=pl.BlockSpec((tm, tn), lambda i, j: (i, j)),
        out_shape=jax.ShapeDtypeStruct((m, n), F32),
        scratch_shapes=[pltpu.VMEM((tm, k), BF16)],
        compiler_params=_cparams(("arbitrary", "arbitrary")),
        name="proj_mm",
    )(x, wp)


def _post_kernel(yq_ref, ys_ref, cosq_ref, sinq_ref, cosk_ref, sink_ref, bf_ref, gq_ref, gkv_ref,
                 sela_ref, selb_ref, wq2_ref, wuk_ref, psel_ref,
                 qa_ref, qi_ref, qb_ref, qc_ref, st_ref, kvb_ref, *, scale_c):
    yq = yq_ref[...]

    def heads(x, sel_ref, out_ref):
        full = jnp.dot(x.astype(BF16), sel_ref[...], preferred_element_type=F32)
        for h in range(N_HEADS):
            out_ref[h] = full[:, h * LANES:(h + 1) * LANES].astype(BF16)

    heads(yq[:, 0:512] * LOG2E, sela_ref, qa_ref)
    heads(yq[:, 512:1024], sela_ref, qi_ref)
    heads(yq[:, 1024:1536] * LOG2E, selb_ref, qb_ref)

    ys = ys_ref[...]
    cq = ys[:, 0:D_CQ]
    cqn = cq * lax.rsqrt(jnp.mean(cq * cq, axis=-1, keepdims=True) + RMS_EPS) * gq_ref[...]
    quq = jnp.dot(cqn.astype(BF16), wq2_ref[...], preferred_element_type=F32)
    x1, x2 = quq[:, 512:640], quq[:, 640:768]
    cos, sin = cosq_ref[...], sinq_ref[...]
    rq = jnp.concatenate([x1 * cos - x2 * sin, x1 * sin + x2 * cos], axis=1) * scale_c
    rsel = jnp.dot(rq.astype(BF16), psel_ref[...], preferred_element_type=F32)
    qlat = jnp.dot(quq[:, 0:512].astype(BF16), wuk_ref[...], preferred_element_type=F32) * scale_c
    for h in range(N_HEADS):
        qc_ref[h, :, 0:LANES] = qlat[:, h * LANES:(h + 1) * LANES].astype(BF16)
        qc_ref[h, :, LANES:2 * LANES] = rsel[:, h * LANES:(h + 1) * LANES].astype(BF16)

    g0 = ys[:, 256:384]
    g1 = ys[:, 384:512]
    g2 = ys[:, 512:640]
    g3 = ys[:, 640:768]
    ckv = ys[:, 768:896]
    ckvn = ckv * lax.rsqrt(jnp.mean(ckv * ckv, axis=-1, keepdims=True) + RMS_EPS) * gkv_ref[...]
    lane = lax.broadcasted_iota(I32, g1.shape, 1)
    swap = jnp.where(lane < L_KROPE + 16, pltpu.roll(g1, LANES - 16, 1), pltpu.roll(g1, 16, 1))
    roped = g1 * cosk_ref[...] + swap * sink_ref[...]
    v = g1 + bf_ref[...]
    logsig = jnp.minimum(v, 0.0) - jnp.log1p(jnp.exp(-jnp.abs(v)))
    g1p = jnp.where((lane >= L_BF) & (lane < L_BF + N_HEADS), logsig, roped)
    v2 = jnp.where(lane < D_HEAD, pltpu.roll(g0, D_HEAD, 1), g0)

    st_ref[:, 0:128] = g0
    st_ref[:, 128:256] = g1p
    st_ref[:, 256:384] = g2
    st_ref[:, 384:512] = g3
    st_ref[:, 512:640] = ckvn
    kvb_ref[:, 0:128] = g0.astype(BF16)
    kvb_ref[:, 128:256] = g1p.astype(BF16)
    kvb_ref[:, 256:384] = g2.astype(BF16)
    kvb_ref[:, 384:512] = g3.astype(BF16)
    kvb_ref[:, 512:640] = ckvn.astype(BF16)
    kvb_ref[:, 640:768] = pltpu.roll(g1p, LANES - L_KROPE, 1).astype(BF16)
    kvb_ref[:, 768:896] = v2.astype(BF16)
    kvb_ref[:, 896:1024] = jnp.zeros_like(g0).astype(BF16)


def _post(y, tabs, consts):
    m = y.shape[0]
    tm = min(m, 256)
    row = lambda w: pl.BlockSpec((tm, w), lambda i: (i, 0))
    full = lambda a: pl.BlockSpec(a.shape, lambda i: (0,) * a.ndim)
    cosq, sinq, cosk, sink = tabs
    bf_row, gq, gkv, sela, selb, wq2, wuk, psel = consts
    hq = lambda w: pl.BlockSpec((N_HEADS, tm, w), lambda i: (0, i, 0))
    kern = functools.partial(_post_kernel, scale_c=float((D_NOPE + D_ROPE) ** -0.5 * LOG2E))
    return pl.pallas_call(
        kern,
        grid=(m // tm,),
        in_specs=[pl.BlockSpec((tm, 1536), lambda i: (i, 0)),
                  pl.BlockSpec((tm, 1024), lambda i: (i, OFF_S // 1024)),
                  row(128), row(128), row(128), row(128),
                  full(bf_row), full(gq), full(gkv), full(sela), full(selb), full(wq2), full(wuk), full(psel)],
        out_specs=[hq(128), hq(128), hq(128), hq(256), row(640), row(1024)],
        out_shape=[jax.ShapeDtypeStruct((N_HEADS, m, 128), BF16),
                   jax.ShapeDtypeStruct((N_HEADS, m, 128), BF16),
                   jax.ShapeDtypeStruct((N_HEADS, m, 128), BF16),
                   jax.ShapeDtypeStruct((N_HEADS, m, 256), BF16),
                   jax.ShapeDtypeStruct((m, 640), F32),
                   jax.ShapeDtypeStruct((m, 1024), BF16)],
        compiler_params=_cparams(("arbitrary",)),
        name="post_proj",
    )(y, y, cosq, sinq, cosk, sink, bf_row, gq, gkv, sela, selb, wq2, wuk, psel)


def _page_prefix_kernel(x_ref, m1_ref, m2_ref, o_ref):
    parts = _split3(x_ref[...])
    w = None
    t = None
    for p in parts:
        dw = jnp.dot(p, m1_ref[...], preferred_element_type=F32)
        dt = jnp.dot(p, m2_ref[...], preferred_element_type=F32)
        w = dw if w is None else w + dw
        t = dt if t is None else t + dt
    o_ref[:, 0:1024] = w
    o_ref[:, 1024:2048] = t


def _page_prefix(x, m1, m2):
    r = x.shape[0]
    tr = min(r, 512)
    assert r % tr == 0
    return pl.pallas_call(
        _page_prefix_kernel,
        grid=(r // tr,),
        in_specs=[pl.BlockSpec((tr, 1024), lambda i: (i, 0)),
                  pl.BlockSpec((1024, 1024), lambda i: (0, 0)),
                  pl.BlockSpec((1024, 1024), lambda i: (0, 0))],
        out_specs=pl.BlockSpec((tr, 2048), lambda i: (i, 0)),
        out_shape=jax.ShapeDtypeStruct((r, 2048), F32),
        compiler_params=_cparams(("arbitrary",)),
        name="page_prefix",
    )(x, m1, m2)


def _blk_prefix_kernel(wt_ref, ltri_ref, o_ref):
    w = wt_ref[:, 0:1024]
    acc = None
    for p in _split3(wt_ref[:, 1024:2048]):
        d = jnp.dot(ltri_ref[...], p, preferred_element_type=F32)
        acc = d if acc is None else acc + d
    o_ref[...] = -(w + acc) * LOG2E


def _blk_prefix(wt, ltri):
    nb = wt.shape[0]
    return pl.pallas_call(
        _blk_prefix_kernel,
        grid=(1,),
        in_specs=[pl.BlockSpec((nb, 2048), lambda i: (0, 0)), pl.BlockSpec((nb, nb), lambda i: (0, 0))],
        out_specs=pl.BlockSpec((nb, 1024), lambda i: (0, 0)),
        out_shape=jax.ShapeDtypeStruct((nb, 1024), F32),
        compiler_params=_cparams(("arbitrary",)),
        name="blk_prefix",
    )(wt, ltri)


def _softmax_update(s_list, v, m_ref, l_ref, acc_ref, tq):
    tk = s_list[0].shape[1]
    hg = N_HEADS // PV_GROUPS
    for g in range(PV_GROUPS):
        ps, alphas = [], []
        for h in range(g * hg, (g + 1) * hg):
            rows = slice(h * tq, (h + 1) * tq)
            s = s_list[h]
            m_prev = m_ref[rows, :]
            m_new = jnp.maximum(m_prev, jnp.max(s, axis=1, keepdims=True))
            alpha = jnp.exp2(m_prev - m_new)
            p = jnp.exp2(s - jnp.concatenate([m_new] * (tk // LANES), axis=1))
            l_ref[rows, :] = alpha * l_ref[rows, :] + jnp.sum(p, axis=1, keepdims=True)
            m_ref[rows, :] = m_new
            ps.append(p.astype(BF16))
            alphas.append(alpha)
        grows = slice(g * hg * tq, (g + 1) * hg * tq)
        pv = jnp.dot(jnp.concatenate(ps, axis=0), v, preferred_element_type=F32)
        acc_ref[grows, :] = jnp.concatenate(alphas, axis=0) * acc_ref[grows, :] + pv


def _init_state(m_ref, l_ref, acc_ref):
    m_ref[...] = jnp.full(m_ref.shape, NEG, F32)
    l_ref[...] = jnp.zeros(l_ref.shape, F32)
    acc_ref[...] = jnp.zeros(acc_ref.shape, F32)


def _float_key(x):
    bits = pltpu.bitcast(x, I32)
    return bits ^ ((bits >> 31) & 0x7FFFFFFF)


def _kth_largest_key(count_ge, k, shape, bits=32):
    zero = jnp.zeros(shape, I32)
    prefix = jnp.where(count_ge(zero) >= k, zero, jnp.full(shape, -(1 << (bits - 1)), I32))

    def bit_body(it, prefix):
        cand = prefix + jnp.left_shift(jnp.int32(1), bits - 2 - it)
        return jnp.where(count_ge(cand) >= k, cand, prefix)

    return lax.fori_loop(0, bits - 1, bit_body, prefix)


def _dsa_prompt_kernel(b31_ref, qi_ref, qa_ref, w_ref, k0_ref, k1_ref, v2_ref, btile_ref, o_ref,
                       key_scr, m_scr, l_scr, acc_scr, *, tq, tk, topk):
    i = pl.program_id(0)
    t0 = i * tq
    nsl = tk // LANES
    qi = qi_ref[...].reshape(N_HEADS * tq, LANES)
    qa = qa_ref[...].reshape(N_HEADS * tq, LANES)
    w = w_ref[...]
    wb = [jnp.broadcast_to(w[:, L_WIDX + h:L_WIDX + h + 1], (tq, tk)) for h in range(N_HEADS)]
    qpos = t0 + lax.broadcasted_iota(I32, (tq, tk), 0)
    lane_k = lax.broadcasted_iota(I32, (tq, tk), 1)
    nc = (t0 + tq + tk - 1) // tk

    def score_body(c, carry):
        k0 = pl.multiple_of(c * tk, tk)
        y = _nt(qi, k1_ref[pl.ds(k0, tk), :])
        acc = wb[0] * jnp.maximum(y[0:tq], 0.0)
        for h in range(1, N_HEADS):
            acc = acc + wb[h] * jnp.maximum(y[h * tq:(h + 1) * tq], 0.0)
        acc = jnp.where(k0 + lane_k <= qpos, acc, -jnp.inf)
        key = _float_key(acc)
        for j in range(nsl):
            key_scr[c * nsl + j] = key[:, j * LANES:(j + 1) * LANES]
        return carry

    lax.fori_loop(0, nc, score_body, 0)

    def count_ge(cand):
        def body(c, cnt):
            for j in range(nsl):
                cnt = cnt + (key_scr[c * nsl + j] >= cand).astype(I32)
            return cnt

        cnt = lax.fori_loop(0, nc, body, jnp.zeros((tq, LANES), I32))
        return jnp.sum(cnt.astype(F32), axis=1, keepdims=True)

    thr = _kth_largest_key(count_ge, float(topk), (tq, LANES))

    _init_state(m_scr, l_scr, acc_scr)
    far_end = jnp.maximum(t0 - tq, 0)
    nf = (far_end + tk - 1) // tk
    thr_k = jnp.concatenate([thr] * nsl, axis=1)

    def far_body(c, carry):
        k0 = pl.multiple_of(c * tk, tk)
        s = _nt(qa, k0_ref[pl.ds(k0, tk), :])
        keyc = jnp.concatenate([key_scr[c * nsl + j] for j in range(nsl)], axis=1)
        sel = (keyc >= thr_k) & (k0 + lane_k < far_end)
        s_list = [jnp.where(sel, s[h * tq:(h + 1) * tq] + b31_ref[h], NEG) for h in range(N_HEADS)]
        _softmax_update(s_list, v2_ref[pl.ds(k0, tk), :], m_scr, l_scr, acc_scr, tq)
        return carry

    lax.fori_loop(0, nf, far_body, 0)

    ns = pl.multiple_of(far_end, tq)
    sl = far_end // LANES
    s = _nt(qa, k0_ref[pl.ds(ns, 2 * tq), :])
    keyw = jnp.concatenate([key_scr[sl], key_scr[sl + 1]], axis=1)
    kposw = ns + lax.broadcasted_iota(I32, (tq, 2 * tq), 1)
    qposw = t0 + lax.broadcasted_iota(I32, (tq, 2 * tq), 0)
    selw = (keyw >= jnp.concatenate([thr, thr], axis=1)) & (kposw <= qposw)
    first = jnp.where(i == 0, 1, 0)
    s_list = []
    for h in range(N_HEADS):
        bias = jnp.concatenate([btile_ref[first, h], btile_ref[1, h]], axis=1)
        s_list.append(jnp.where(selw, s[h * tq:(h + 1) * tq] + bias, NEG))
    _softmax_update(s_list, v2_ref[pl.ds(ns, 2 * tq), :], m_scr, l_scr, acc_scr, tq)

    lane = lax.broadcasted_iota(I32, (tq, LANES), 1)
    for c in range(N_HEADS // 2):
        r0 = slice(2 * c * tq, (2 * c + 1) * tq)
        r1 = slice((2 * c + 1) * tq, (2 * c + 2) * tq)
        o_ref[:, c * LANES:(c + 1) * LANES] = jnp.where(
            lane < D_HEAD, acc_scr[r0, :] / l_scr[r0, :], acc_scr[r1, :] / l_scr[r1, :])


def _dsa_prompt(b31, qi, qa, st, kvb, btile, topk):
    s = st.shape[0]
    tq, tk = 128, min(512, s)
    assert s % tk == 0 and s >= 2 * tq
    hq = pl.BlockSpec((N_HEADS, tq, 128), lambda i: (0, i, 0))
    col = lambda j: pl.BlockSpec((s, 128), lambda i: (0, j))
    kern = functools.partial(_dsa_prompt_kernel, tq=tq, tk=tk, topk=topk)
    return pl.pallas_call(
        kern,
        grid=(s // tq,),
        in_specs=[pl.BlockSpec(memory_space=pltpu.SMEM),
                  hq, hq,
                  pl.BlockSpec((tq, 128), lambda i: (i, 1)),
                  col(0), col(1), col(6),
                  pl.BlockSpec(btile.shape, lambda i: (0, 0, 0, 0))],
        out_specs=pl.BlockSpec((tq, BRANCH_W), lambda i: (i, 0)),
        out_shape=jax.ShapeDtypeStruct((s, BRANCH_W), F32),
        scratch_shapes=[pltpu.VMEM((s // LANES, tq, LANES), I32),
                        pltpu.VMEM((N_HEADS * tq, LANES), F32),
                        pltpu.VMEM((N_HEADS * tq, LANES), F32),
                        pltpu.VMEM((N_HEADS * tq, LANES), F32)],
        compiler_params=_cparams(("arbitrary",)),
        name="dsa_prompt",
    )(b31, qi, qa, st, kvb, kvb, kvb, btile)


def _flash_prompt_kernel(q_ref, kv_ref, aux_ref, o_ref, m_scr, l_scr, acc_scr, *, tq, tk, mode):
    i = pl.program_id(0)
    t0 = i * tq
    dq = q_ref.shape[2]
    q = q_ref[...].reshape(N_HEADS * tq, dq)
    nfull = t0 // tk
    _init_state(m_scr, l_scr, acc_scr)

    def kv(k0):
        if mode == "fox":
            return kv_ref[pl.ds(k0, tk), 0:LANES], kv_ref[pl.ds(k0, tk), LANES:2 * LANES]
        k = kv_ref[pl.ds(k0, tk), :]
        return k, k[:, 0:LANES]

    def logits(c, k):
        s = _nt(q, k)
        out = []
        for h in range(N_HEADS):
            sh = s[h * tq:(h + 1) * tq]
            if mode == "fox":
                sh = sh + aux_ref[c, h:h + 1, :]
            out.append(sh)
        return out

    def full_body(c, carry):
        k, v = kv(pl.multiple_of(c * tk, tk))
        _softmax_update(logits(c, k), v, m_scr, l_scr, acc_scr, tq)
        return carry

    lax.fori_loop(0, nfull, full_body, 0)

    k0 = pl.multiple_of(nfull * tk, tk)
    k, v = kv(k0)
    causal = (k0 + lax.broadcasted_iota(I32, (tq, tk), 1)) <= (t0 + lax.broadcasted_iota(I32, (tq, tk), 0))
    s_list = [jnp.where(causal, sh, NEG) for sh in logits(nfull, k)]
    _softmax_update(s_list, v, m_scr, l_scr, acc_scr, tq)

    if mode == "fox":
        lane = lax.broadcasted_iota(I32, (tq, LANES), 1)
        for c in range(N_HEADS // 2):
            r0 = slice(c * tq, (c + 1) * tq)
            r1 = slice((c + 4) * tq, (c + 5) * tq)
            o_ref[:, c * LANES:(c + 1) * LANES] = jnp.where(
                lane < D_HEAD, acc_scr[r0, :] / l_scr[r0, :], acc_scr[r1, :] / l_scr[r1, :])
    else:
        out = None
        for h in range(N_HEADS):
            rows = slice(h * tq, (h + 1) * tq)
            olat = (acc_scr[rows, :] / l_scr[rows, :]).astype(BF16)
            d = jnp.dot(olat, aux_ref[h], preferred_element_type=F32)
            out = d if out is None else out + d
        o_ref[...] = out


def _flash_prompt(q, kvb, aux, mode):
    s = kvb.shape[0]
    tq, tk = 128, min(FLASH_TK, s)
    dq = q.shape[2]
    kern = functools.partial(_flash_prompt_kernel, tq=tq, tk=tk, mode=mode)
    return pl.pallas_call(
        kern,
        grid=(s // tq,),
        in_specs=[pl.BlockSpec((N_HEADS, tq, dq), lambda i: (0, i, 0)),
                  pl.BlockSpec((s, 256), lambda i: (0, 1 if mode == "fox" else 2)),
                  pl.BlockSpec(aux.shape, lambda i: (0, 0, 0))],
        out_specs=pl.BlockSpec((tq, BRANCH_W), lambda i: (i, 0)),
        out_shape=jax.ShapeDtypeStruct((s, BRANCH_W), F32),
        scratch_shapes=[pltpu.VMEM((N_HEADS * tq, LANES), F32),
                        pltpu.VMEM((N_HEADS * tq, LANES), F32),
                        pltpu.VMEM((N_HEADS * tq, LANES), F32)],
        compiler_params=_cparams(("arbitrary",)),
        name=mode + "_prompt",
    )(q, kvb, aux)


def _merge_kernel(oa_ref, ob_ref, oc_ref, z_ref, g_ref, x_ref, wbr_ref, wout_ref, lng_ref, lnb_ref, o_ref,
                  *, alpha):
    mixed = None
    for n, o in enumerate((oa_ref, ob_ref, oc_ref)):
        zn = z_ref[:, n * BRANCH_W:(n + 1) * BRANCH_W]
        u = o[...] * (zn * jax.nn.sigmoid(zn))
        y = jnp.dot(u.astype(BF16), wbr_ref[n], preferred_element_type=F32)
        t = jax.nn.sigmoid(g_ref[:, n * D_MODEL:(n + 1) * D_MODEL]) * y
        mixed = t if mixed is None else mixed + t
    out = jnp.dot(mixed.astype(BF16), wout_ref[...], preferred_element_type=F32)
    hres = alpha * x_ref[...] + out
    mu = jnp.mean(hres, axis=-1, keepdims=True)
    d = hres - mu
    var = jnp.mean(d * d, axis=-1, keepdims=True)
    o_ref[...] = d * lax.rsqrt(var + LN_EPS) * lng_ref[...] + lnb_ref[...]


def _merge(oa, ob, oc, y, x, wbr, wout, lng, lnb, alpha):
    m = x.shape[0]
    tm = min(m, 256)
    row = lambda w: pl.BlockSpec((tm, w), lambda i: (i, 0))
    full = lambda a: pl.BlockSpec(a.shape, lambda i: (0,) * a.ndim)
    return pl.pallas_call(
        functools.partial(_merge_kernel, alpha=alpha),
        grid=(m // tm,),
        in_specs=[row(BRANCH_W), row(BRANCH_W), row(BRANCH_W),
                  pl.BlockSpec((tm, 1536), lambda i: (i, OFF_Z // 1536)),
                  pl.BlockSpec((tm, 3072), lambda i: (i, OFF_G // 3072)),
                  row(D_MODEL), full(wbr), full(wout), full(lng), full(lnb)],
        out_specs=row(D_MODEL),
        out_shape=jax.ShapeDtypeStruct((m, D_MODEL), F32),
        compiler_params=_cparams(("arbitrary",)),
        name="merge",
    )(oa, ob, oc, y, y, x, wbr, wout, lng, lnb)


def _np_consts():
    sela = np.zeros((512, 1024), np.float32)
    selb = np.zeros((512, 1024), np.float32)
    for h in range(N_HEADS):
        for d in range(D_HEAD):
            sela[h * 64 + d, h * 128 + d] = 1.0
            selb[h * 64 + d, h * 128 + (0 if h < 4 else 64) + d] = 1.0
    psel = np.zeros((256, 1024), np.float32)
    for h in range(N_HEADS):
        for j in range(16):
            psel[h * 16 + j, h * 128 + j] = 1.0
            psel[128 + h * 16 + j, h * 128 + 16 + j] = 1.0
    i = np.arange(1024)
    r, h = i // 8, i % 8
    hp, rp = i // 128, i % 128
    m2 = (h[:, None] == hp[None, :])
    m1 = m2 & (r[:, None] <= rp[None, :])
    return sela, selb, psel, m1.astype(np.float32), m2.astype(np.float32)


def _t5_bucket(dist):
    max_exact = N_BUCKETS // 2
    d = jnp.maximum(dist, 1).astype(F32)
    large = max_exact + (jnp.log(d / max_exact) / np.log(MAX_DISTANCE / max_exact)
                         * (N_BUCKETS - max_exact)).astype(I32)
    large = jnp.minimum(large, N_BUCKETS - 1)
    return jnp.where(dist < max_exact, dist, large)


def _rope_tables(pos):
    half = D_ROPE // 2
    freqs = ROPE_THETA ** (-jnp.arange(half, dtype=F32) / half)
    ang = pos.astype(F32)[:, None] * freqs
    cos, sin = jnp.cos(ang), jnp.sin(ang)
    m = pos.shape[0]
    cosq, sinq = jnp.tile(cos, (1, N_HEADS)), jnp.tile(sin, (1, N_HEADS))
    cosk = jnp.ones((m, LANES), F32).at[:, L_KROPE:L_KROPE + 16].set(cos).at[:, L_KROPE + 16:L_KROPE + 32].set(cos)
    sink = jnp.zeros((m, LANES), F32).at[:, L_KROPE:L_KROPE + 16].set(-sin).at[:, L_KROPE + 16:L_KROPE + 32].set(sin)
    return cosq, sinq, cosk, sink


def _prep_layer(l, w_in, b_forget, w_c_uq, g_c_q, g_c_kv, w_c_ukv, w_branch, w_out, ln_g, ln_b):
    sela, selb, psel, _, _ = _np_consts()
    src, scale = _col_perm()
    wp = (jnp.take(w_in[l], jnp.asarray(src), axis=1) * jnp.asarray(scale)).astype(BF16)
    bf_row = jnp.zeros((1, LANES), F32).at[0, L_BF:L_BF + N_HEADS].set(b_forget[l])
    wq = w_c_uq[l]
    wq2 = jnp.concatenate([wq[:, :, :D_NOPE].reshape(D_CQ, 512),
                           wq[:, :, D_NOPE:D_NOPE + 16].reshape(D_CQ, 128),
                           wq[:, :, D_NOPE + 16:].reshape(D_CQ, 128)], axis=1).astype(BF16)
    w_uk, w_uv = w_c_ukv[l][..., :D_NOPE], w_c_ukv[l][..., D_NOPE:]
    eye = jnp.eye(N_HEADS, dtype=F32)
    wukbd = jnp.einsum('chn,hg->hngc', w_uk, eye).reshape(512, 1024).astype(BF16)
    wuvpad = jnp.einsum('chv,hg->hcgv', w_uv, eye).reshape(N_HEADS, D_LATENT, 512).astype(BF16)
    wbr = w_branch[l].at[1].set(w_branch[l][1][jnp.asarray(_fox_head_perm())]).astype(BF16)
    return dict(
        wp=wp, wuvpad=wuvpad, wbr=wbr, wout=w_out[l].astype(BF16), lng=ln_g[l][None], lnb=ln_b[l][None],
        post=(bf_row, g_c_q[l][None], g_c_kv[l][None], jnp.asarray(sela, BF16), jnp.asarray(selb, BF16),
              wq2, wukbd, jnp.asarray(psel, BF16)))


def _bias_tables(t5_table):
    r = jnp.arange(PAGE)[:, None]
    c = jnp.arange(PAGE)[None, :]
    dist = jnp.stack([PAGE + r - c, jnp.maximum(r - c, 0)])
    btile = jnp.transpose(t5_table[_t5_bucket(dist)], (0, 3, 1, 2))
    bfar = t5_table[_t5_bucket(jnp.asarray(2 * PAGE))]
    return btile.astype(F32) * LOG2E, bfar.astype(F32) * LOG2E


def _prompt_layer(xp, prep, tabs, btile, bfar, topk, alpha, m1, m2):
    s = xp.shape[0]
    y = _project(xp, prep['wp'])
    qa, qi, qb, qc, st, kvb = _post(y, tabs, prep['post'])
    nb = s // PAGE
    wt = _page_prefix(st[:, 128 + L_BF:128 + L_BF + N_HEADS].reshape(nb, 1024), m1, m2)
    ltri = jnp.asarray(np.tril(np.ones((nb, nb), np.float32), -1), BF16)
    nf = _blk_prefix(wt, ltri)
    tk = min(FLASH_TK, s)
    nf = jnp.transpose(nf.reshape(nb, N_HEADS, PAGE), (1, 0, 2)).reshape(N_HEADS, s // tk, tk)
    nf = jnp.transpose(nf, (1, 0, 2))
    o_a = _dsa_prompt(bfar, qi, qa, st, kvb, btile, topk)
    o_b = _flash_prompt(qb, kvb, nf, "fox")
    o_c = _flash_prompt(qc, kvb, prep['wuvpad'], "mla")
    xp = _merge(o_a, o_b, o_c, y, xp, prep['wbr'], prep['wout'], prep['lng'], prep['lnb'], alpha)
    return xp, st


def _state_rows(st, lead):
    f = lambda a, *tail: a.reshape(lead + tail)
    return (f(st[:, 0:64], 64), f(st[:, 64:128], 64), f(st[:, 128:192], 64),
            f(st[:, 256:384], KV_B, D_HEAD), f(st[:, 384:512], KV_B, D_HEAD),
            f(st[:, 128 + L_BF:128 + L_BF + N_HEADS], N_HEADS),
            f(st[:, 512:640], D_LATENT), f(st[:, 128 + L_KROPE:128 + L_KROPE + D_ROPE], D_ROPE))


def _page_pipeline(pt_ref, caches, bufs, sems, layer, pps):
    b, c = pl.program_id(0), pl.program_id(1)
    nb, nch = pl.num_programs(0), pl.num_programs(1)
    n = b * nch + c
    slot = lax.rem(n, 2)

    def copies(bb, cc, sl):
        out = []
        for cache, buf, sem in zip(caches, bufs, sems):
            for p in range(pps):
                page = pt_ref[bb, cc * pps + p]
                out.append(pltpu.make_async_copy(cache.at[layer, page], buf.at[sl, p], sem.at[sl]))
        return out

    @pl.when(n == 0)
    def _():
        for cp in copies(b, c, slot):
            cp.start()

    @pl.when(n + 1 < nb * nch)
    def _():
        wrap = c + 1 == nch
        for cp in copies(jnp.where(wrap, b + 1, b), jnp.where(wrap, 0, c + 1), 1 - slot):
            cp.start()

    for cp in copies(b, c, slot):
        cp.wait()
    return slot


def _page_scratch(page_shapes, pps):
    return ([pltpu.VMEM((2, pps) + tuple(ps), F32) for ps in page_shapes]
            + [pltpu.SemaphoreType.DMA((2,)) for _ in page_shapes])


_ANY = pl.BlockSpec(memory_space=pl.ANY)


def _pages_t(cache):
    nd = cache.ndim
    return jnp.transpose(cache, (0, 1) + tuple(range(3, nd)) + (2,))


def _seq_spec(*shape):
    nd = len(shape)
    return pl.BlockSpec((None,) + shape, lambda b, c, pt: (b,) + (0,) * nd)


def _tile_lanes(x, n):
    return jnp.concatenate([x] * n, axis=1) if n > 1 else x


def _dec_update(s, pv_fn, m_ref, l_ref, acc_ref):
    dv = acc_ref.shape[1]
    m_prev = m_ref[...]
    m_new = jnp.maximum(m_prev, jnp.max(s, axis=1, keepdims=True))
    alpha = jnp.exp2(m_prev - m_new)
    p = jnp.exp2(s - _tile_lanes(m_new, s.shape[1] // LANES))
    l_ref[...] = alpha * l_ref[...] + jnp.sum(p, axis=1, keepdims=True)
    m_ref[...] = m_new
    acc_ref[...] = alpha[:, 0:dv] * acc_ref[...] + pv_fn(p.astype(BF16))


def _dec_self(s_self, v_self, m_ref, l_ref, acc_ref):
    dv = acc_ref.shape[1]
    m_prev = m_ref[...]
    m_new = jnp.maximum(m_prev, s_self)
    alpha = jnp.exp2(m_prev - m_new)
    p = jnp.exp2(s_self - m_new)
    l_ref[...] = alpha * l_ref[...] + p
    m_ref[...] = m_new
    acc_ref[...] = alpha[:, 0:dv] * acc_ref[...] + p[:, 0:dv] * v_self


def _idx_sample_kernel(pt_ref, q_ref, w_ref, g1_ref, kidx_hbm, sc_ref, self_ref, kbuf, ksem, *, pps, layer):
    slot = _page_pipeline(pt_ref, [kidx_hbm], [kbuf], [ksem], layer, pps)
    q = q_ref[...]
    w = w_ref[...]
    kt = jnp.concatenate([kbuf[slot, p] for p in range(pps)], axis=1).astype(BF16)
    y = jnp.dot(q[:, 0:D_HEAD], kt, preferred_element_type=F32)
    sc_ref[...] = jnp.sum(jnp.maximum(y, 0.0) * _tile_lanes(w, pps), axis=0, keepdims=True)

    @pl.when(pl.program_id(1) == 0)
    def _():
        ys = jnp.sum(q.astype(F32) * g1_ref[...], axis=1, keepdims=True)
        val = jnp.sum(jnp.maximum(ys, 0.0) * w[:, 0:1], axis=0, keepdims=True)
        lane = lax.broadcasted_iota(I32, (1, LANES), 1)
        self_ref[...] = jnp.where(lane == 0, val, -jnp.inf)


def _idx_sample(page_table, layer, qi_t, w8, g1new, cache_kidx, pps):
    b, npages = page_table.shape
    nch = npages // pps
    grid_spec = pltpu.PrefetchScalarGridSpec(
        num_scalar_prefetch=1, grid=(b, nch),
        in_specs=[_seq_spec(N_HEADS, LANES), _seq_spec(N_HEADS, LANES), _seq_spec(1, LANES), _ANY],
        out_specs=[pl.BlockSpec((None, 1, PAGE * pps), lambda b, c, pt: (b, 0, c)), _seq_spec(1, LANES)],
        scratch_shapes=_page_scratch([(D_HEAD, PAGE)], pps))
    return pl.pallas_call(
        functools.partial(_idx_sample_kernel, pps=pps, layer=layer), grid_spec=grid_spec,
        out_shape=[jax.ShapeDtypeStruct((b, 1, npages * PAGE), F32), jax.ShapeDtypeStruct((b, 1, LANES), F32)],
        compiler_params=_cparams(("arbitrary", "arbitrary")), name="idx_sample",
    )(page_table, qi_t, w8, g1new, cache_kidx)


def _thresh_kernel(sc_ref, o_ref, key_scr, *, topk):
    rows = sc_ref.shape[0]
    nsl = sc_ref.shape[1] // LANES
    for j in range(nsl):
        key_scr[j] = _float_key(sc_ref[:, j * LANES:(j + 1) * LANES])

    def count_ge(cand):
        cnt = lax.fori_loop(0, nsl, lambda j, cnt: cnt + (key_scr[j] >= cand).astype(I32),
                            jnp.zeros((rows, LANES), I32))
        return jnp.sum(cnt.astype(F32), axis=1, keepdims=True)

    thr = _kth_largest_key(count_ge, float(topk), (rows, LANES))
    bits = thr ^ ((thr >> 31) & 0x7FFFFFFF)
    o_ref[...] = jnp.where(thr == INT_MIN, -jnp.inf, pltpu.bitcast(bits, F32))


def _thresh(sc, topk):
    rows, n = sc.shape
    return pl.pallas_call(
        functools.partial(_thresh_kernel, topk=topk), grid=(1,),
        in_specs=[pl.BlockSpec((rows, n), lambda i: (0, 0))],
        out_specs=pl.BlockSpec((rows, LANES), lambda i: (0, 0)),
        out_shape=jax.ShapeDtypeStruct((rows, LANES), F32),
        scratch_shapes=[pltpu.VMEM((n // LANES, rows, LANES), I32)],
        compiler_params=_cparams(("arbitrary",)), name="thresh_sample",
    )(sc)


def _dsa_sample_kernel(pt_ref, q_ref, sc_ref, self_ref, thr_ref, g0_ref, bfar_ref, blast_ref, bself_ref,
                       k_hbm, v_hbm, o_ref, m_scr, l_scr, acc_scr, kbuf, vbuf, ksem, vsem, *, pps, layer):
    slot = _page_pipeline(pt_ref, [k_hbm, v_hbm], [kbuf, vbuf], [ksem, vsem], layer, pps)
    c = pl.program_id(1)
    last = c == pl.num_programs(1) - 1

    @pl.when(c == 0)
    def _():
        _init_state(m_scr, l_scr, acc_scr)

    q = q_ref[...]
    kt = jnp.concatenate([kbuf[slot, p] for p in range(pps)], axis=1).astype(BF16)
    vt = jnp.concatenate([vbuf[slot, p] for p in range(pps)], axis=1).astype(BF16)
    thr = thr_ref[...]
    sel = sc_ref[...] >= _tile_lanes(thr, pps)
    bfar = bfar_ref[...]
    bias = jnp.concatenate([bfar] * (pps - 1) + [jnp.where(last, blast_ref[...], bfar)], axis=1)
    s = jnp.where(sel, jnp.dot(q[:, 0:D_HEAD], kt, preferred_element_type=F32) + bias, NEG)
    _dec_update(s, lambda p: _nt(p, vt), m_scr, l_scr, acc_scr)

    @pl.when(last)
    def _():
        g0 = g0_ref[...]
        s_self = jnp.sum(q.astype(F32) * g0, axis=1, keepdims=True) + bself_ref[...]
        s_self = jnp.where(self_ref[:, 0:1] >= thr[:, 0:1], s_self, NEG)
        _dec_self(s_self, g0[:, D_HEAD:2 * D_HEAD], m_scr, l_scr, acc_scr)
        o_ref[...] = acc_scr[...] / l_scr[:, 0:D_HEAD]


def _dsa_sample(page_table, layer, qa_t, sc, selfsc, thr, g0new, bfar8, blast, bself8, cache_k, cache_v, pps):
    b, npages = page_table.shape
    full = lambda a: pl.BlockSpec(a.shape, lambda b, c, pt: (0,) * a.ndim)
    grid_spec = pltpu.PrefetchScalarGridSpec(
        num_scalar_prefetch=1, grid=(b, npages // pps),
        in_specs=[_seq_spec(N_HEADS, LANES), pl.BlockSpec((None, 1, PAGE * pps), lambda b, c, pt: (b, 0, c)),
                  _seq_spec(1, LANES), _seq_spec(1, LANES), _seq_spec(1, LANES), full(bfar8), full(blast), full(bself8),
                  _ANY, _ANY],
        out_specs=_seq_spec(N_HEADS, D_HEAD),
        scratch_shapes=[pltpu.VMEM((N_HEADS, LANES), F32), pltpu.VMEM((N_HEADS, LANES), F32),
                        pltpu.VMEM((N_HEADS, D_HEAD), F32)] + _page_scratch([(D_HEAD, PAGE)] * 2, pps))
    return pl.pallas_call(
        functools.partial(_dsa_sample_kernel, pps=pps, layer=layer), grid_spec=grid_spec,
        out_shape=jax.ShapeDtypeStruct((b, N_HEADS, D_HEAD), F32),
        compiler_params=_cparams(("arbitrary", "arbitrary")), name="dsa_sample",
    )(page_table, qa_t, sc, selfsc, thr, g0new, bfar8, blast, bself8, cache_k, cache_v)


def _fox_sample_kernel(pt_ref, q_ref, st_ref, lf_ref, utri_ref, ones_ref, k_hbm, v_hbm, f_hbm,
                       o_ref, m_scr, l_scr, acc_scr, carry_scr, kbuf, vbuf, fbuf, ksem, vsem, fsem, *, pps, layer):
    slot = _page_pipeline(pt_ref, [k_hbm, v_hbm, f_hbm], [kbuf, vbuf, fbuf], [ksem, vsem, fsem], layer, pps)
    kp = [kbuf.at[slot, p] for p in range(pps)]
    vp = [vbuf.at[slot, p] for p in range(pps)]
    fp = [fbuf.at[slot, p] for p in range(pps)]
    c = pl.program_id(1)

    @pl.when(c == 0)
    def _():
        _init_state(m_scr, l_scr, acc_scr)
        carry_scr[...] = jnp.zeros(carry_scr.shape, F32)

    q = q_ref[...]
    q0, q1 = q[:, 0:D_HEAD], q[:, D_HEAD:2 * D_HEAD]
    kv = lambda pages, g: jnp.concatenate([p[g] for p in pages], axis=1).astype(BF16)
    s = (jnp.dot(q0, kv(kp, 0), preferred_element_type=F32)
         + jnp.dot(q1, kv(kp, 1), preferred_element_type=F32))
    parts = _split3(jnp.concatenate([p[...] for p in fp], axis=0))
    wsum = tsum = None
    for part in parts:
        dw = jnp.dot(part, utri_ref[...], preferred_element_type=F32)
        dt = jnp.dot(part, ones_ref[...], preferred_element_type=F32)
        wsum = dw if wsum is None else wsum + dw
        tsum = dt if tsum is None else tsum + dt
    po = carry_scr[...]
    biases = []
    for p in range(pps):
        biases.append(wsum[p * N_HEADS:(p + 1) * N_HEADS, :] + po)
        po = po + tsum[p * N_HEADS:(p + 1) * N_HEADS, :]
    carry_scr[...] = po
    s = s - jnp.concatenate(biases, axis=1) * LOG2E
    v0, v1 = kv(vp, 0), kv(vp, 1)
    row = lax.broadcasted_iota(I32, (N_HEADS, D_HEAD), 0)
    lower = row < N_HEADS // KV_B
    _dec_update(s, lambda p: jnp.where(lower, _nt(p, v0), _nt(p, v1)), m_scr, l_scr, acc_scr)

    @pl.when(c == pl.num_programs(1) - 1)
    def _():
        k_new, v_new = st_ref[:, 256:384], st_ref[:, 384:512]
        s_self = jnp.sum(q.astype(F32) * k_new, axis=1, keepdims=True) - (po + lf_ref[...]) * LOG2E
        v_self = jnp.where(lower, jnp.broadcast_to(v_new[:, 0:D_HEAD], (N_HEADS, D_HEAD)),
                           jnp.broadcast_to(v_new[:, D_HEAD:2 * D_HEAD], (N_HEADS, D_HEAD)))
        _dec_self(s_self, v_self, m_scr, l_scr, acc_scr)
        o_ref[...] = acc_scr[...] / l_scr[:, 0:D_HEAD]


def _fox_sample(page_table, layer, qb_t, st3, lf8, cache_k, cache_v, cache_f, pps):
    b, npages = page_table.shape
    utri = jnp.asarray(np.triu(np.ones((PAGE, PAGE), np.float32)), BF16)
    ones = jnp.ones((PAGE, PAGE), BF16)
    full = lambda a: pl.BlockSpec(a.shape, lambda b, c, pt: (0,) * a.ndim)
    grid_spec = pltpu.PrefetchScalarGridSpec(
        num_scalar_prefetch=1, grid=(b, npages // pps),
        in_specs=[_seq_spec(N_HEADS, LANES), _seq_spec(1, 640), _seq_spec(N_HEADS, LANES), full(utri), full(ones),
                  _ANY, _ANY, _ANY],
        out_specs=_seq_spec(N_HEADS, D_HEAD),
        scratch_shapes=[pltpu.VMEM((N_HEADS, LANES), F32), pltpu.VMEM((N_HEADS, LANES), F32),
                        pltpu.VMEM((N_HEADS, D_HEAD), F32), pltpu.VMEM((N_HEADS, LANES), F32)]
        + _page_scratch([(KV_B, D_HEAD, PAGE), (KV_B, D_HEAD, PAGE), (N_HEADS, PAGE)], pps))
    return pl.pallas_call(
        functools.partial(_fox_sample_kernel, pps=pps, layer=layer), grid_spec=grid_spec,
        out_shape=jax.ShapeDtypeStruct((b, N_HEADS, D_HEAD), F32),
        compiler_params=_cparams(("arbitrary", "arbitrary")), name="fox_sample",
    )(page_table, qb_t, st3, lf8, utri, ones, cache_k, cache_v, cache_f)


def _mla_sample_kernel(pt_ref, q_ref, kn_ref, c_hbm, r_hbm, o_ref, m_scr, l_scr, acc_scr,
                       cbuf, rbuf, csem, rsem, *, pps, layer):
    slot = _page_pipeline(pt_ref, [c_hbm, r_hbm], [cbuf, rbuf], [csem, rsem], layer, pps)
    c = pl.program_id(1)

    @pl.when(c == 0)
    def _():
        _init_state(m_scr, l_scr, acc_scr)

    q = q_ref[...]
    ckv = jnp.concatenate([cbuf[slot, p] for p in range(pps)], axis=0).astype(BF16)
    krt = jnp.concatenate([rbuf[slot, p] for p in range(pps)], axis=1).astype(BF16)
    s = _nt(q[:, 0:D_LATENT], ckv) + jnp.dot(q[:, D_LATENT:D_LATENT + D_ROPE], krt, preferred_element_type=F32)
    _dec_update(s, lambda p: jnp.dot(p, ckv, preferred_element_type=F32), m_scr, l_scr, acc_scr)

    @pl.when(c == pl.num_programs(1) - 1)
    def _():
        kn = kn_ref[...].astype(F32)
        s_self = jnp.sum(q.astype(F32) * kn, axis=1, keepdims=True)
        _dec_self(s_self, kn[:, 0:D_LATENT], m_scr, l_scr, acc_scr)
        o_ref[...] = acc_scr[...] / l_scr[...]


def _mla_sample(page_table, layer, qc_t, kvb3, cache_lat, cache_kr, pps):
    b, npages = page_table.shape
    grid_spec = pltpu.PrefetchScalarGridSpec(
        num_scalar_prefetch=1, grid=(b, npages // pps),
        in_specs=[_seq_spec(N_HEADS, 2 * LANES), pl.BlockSpec((None, 1, 2 * LANES), lambda b, c, pt: (b, 0, 2)),
                  _ANY, _ANY],
        out_specs=_seq_spec(N_HEADS, D_LATENT),
        scratch_shapes=[pltpu.VMEM((N_HEADS, LANES), F32), pltpu.VMEM((N_HEADS, LANES), F32),
                        pltpu.VMEM((N_HEADS, D_LATENT), F32)]
        + _page_scratch([(PAGE, D_LATENT), (D_ROPE, PAGE)], pps))
    return pl.pallas_call(
        functools.partial(_mla_sample_kernel, pps=pps, layer=layer), grid_spec=grid_spec,
        out_shape=jax.ShapeDtypeStruct((b, N_HEADS, D_LATENT), F32),
        compiler_params=_cparams(("arbitrary", "arbitrary")), name="mla_sample",
    )(page_table, qc_t, kvb3, cache_lat, cache_kr)


def _uv_kernel(o_ref, w_ref, out_ref):
    out = None
    for h in range(N_HEADS):
        d = jnp.dot(o_ref[h].astype(BF16), w_ref[h], preferred_element_type=F32)
        out = d if out is None else out + d
    out_ref[...] = out


def _uv_proj(olat_t, wuvpad):
    b = olat_t.shape[1]
    return pl.pallas_call(
        _uv_kernel, grid=(1,),
        in_specs=[pl.BlockSpec(olat_t.shape, lambda i: (0, 0, 0)), pl.BlockSpec(wuvpad.shape, lambda i: (0, 0, 0))],
        out_specs=pl.BlockSpec((b, BRANCH_W), lambda i: (0, 0)),
        out_shape=jax.ShapeDtypeStruct((b, BRANCH_W), F32),
        compiler_params=_cparams(("arbitrary",)), name="uv_proj",
    )(olat_t, wuvpad)


def _sample_layer(xs, layer, prep, tabs, caches, page_table, bias_s, topk, alpha, pps):
    ca_k, ca_v, ca_kidx, cb_k, cb_v, cb_f, cc_lat, cc_kr = caches
    bfar8, blast, bself8 = bias_s
    b = xs.shape[0]
    y = _project(xs, prep['wp'])
    qa, qi, qb, qc, st, kvb = _post(y, tabs, prep['post'])
    tr = lambda q: jnp.transpose(q, (1, 0, 2))
    g1 = st[:, 128:256]
    bc8 = lambda a: jnp.broadcast_to(a[:, :, None], (b, N_HEADS, LANES))
    w8 = bc8(g1[:, L_WIDX:L_WIDX + N_HEADS])
    lf8 = bc8(g1[:, L_BF:L_BF + N_HEADS])
    st3 = st[:, None, :]
    sc, selfsc = _idx_sample(page_table, layer, tr(qi), w8, g1[:, None, :], ca_kidx, pps)
    thr = _thresh(jnp.concatenate([sc[:, 0, :], selfsc[:, 0, :]], axis=1), topk)
    o_a = _dsa_sample(page_table, layer, tr(qa), sc, selfsc, thr[:, None, :], st3[:, :, 0:128], bfar8, blast, bself8,
                      ca_k, ca_v, pps)
    o_b = _fox_sample(page_table, layer, tr(qb), st3, lf8, cb_k, cb_v, cb_f, pps)
    olat = _mla_sample(page_table, layer, tr(qc), kvb[:, None, :], cc_lat, cc_kr, pps)
    o_c = _uv_proj(tr(olat), prep['wuvpad'])
    o_b = o_b.reshape(b, BRANCH_W)[:, jnp.asarray(_fox_head_perm())]
    xs = _merge(o_a.reshape(b, BRANCH_W), o_b, o_c, y, xs, prep['wbr'], prep['wout'], prep['lng'], prep['lnb'], alpha)
    return xs, st


def kernel(x_prompt, x_sample, cache_a_k, cache_a_v, cache_a_kidx, cache_b_k, cache_b_v, cache_b_logf,
           cache_c_latent, cache_c_krope, page_table, t5_table, w_in, b_forget, w_c_uq, g_c_q, g_c_kv,
           w_c_ukv, w_branch, w_out, ln_g, ln_b):
    depth = w_in.shape[0]
    bp, s, _ = x_prompt.shape
    bs, ds, _ = x_sample.shape
    npages = page_table.shape[1]
    past = npages * PAGE
    n_pool = cache_a_k.shape[1]
    assert bp == 1 and ds == 1 and cache_a_k.shape[2] == PAGE
    alpha = float((2 * depth) ** 0.25)
    pps = 64 if npages % 64 == 0 else npages // 2
    _, _, _, m1, m2 = _np_consts()
    m1, m2 = jnp.asarray(m1, BF16), jnp.asarray(m2, BF16)
    btile, bfar = _bias_tables(t5_table)
    bc = lambda a: jnp.broadcast_to(a[:, None], (N_HEADS, LANES)).astype(F32)
    dist_last = PAGE - jnp.arange(PAGE)
    bias_s = (bc(bfar), jnp.transpose(t5_table[_t5_bucket(dist_last)]).astype(F32) * LOG2E,
              bc(t5_table[_t5_bucket(jnp.asarray(0))]) * LOG2E)
    tabs_p = _rope_tables(jnp.arange(s, dtype=I32))
    tabs_s = _rope_tables(jnp.full((bs,), past, I32))
    caches = (_pages_t(cache_a_k), _pages_t(cache_a_v), _pages_t(cache_a_kidx), _pages_t(cache_b_k),
              _pages_t(cache_b_v), _pages_t(cache_b_logf), cache_c_latent, _pages_t(cache_c_krope))
    xp = x_prompt.reshape(s, D_MODEL)
    xs = x_sample.reshape(bs, D_MODEL)
    rows_p, rows_s = [], []
    for l in range(depth):
        prep = _prep_layer(l, w_in, b_forget, w_c_uq, g_c_q, g_c_kv, w_c_ukv, w_branch, w_out, ln_g, ln_b)
        xp, st_p = _prompt_layer(xp, prep, tabs_p, btile, bfar, min(TOPK_MAX, s // 4), alpha, m1, m2)
        rows_p.append(_state_rows(st_p, (bp, s)))
        xs, st_s = _sample_layer(xs, l, prep, tabs_s, caches, page_table, bias_s,
                                 min(TOPK_MAX, (past + 1) // 4), alpha, pps)
        rows_s.append(_state_rows(st_s, (bs, 1)))
    new_p = [jnp.stack([r[i] for r in rows_p], axis=0) for i in range(8)]
    new_s = [jnp.stack([r[i] for r in rows_s], axis=0) for i in range(8)]
    return (xp.reshape(bp, s, D_MODEL), xs.reshape(bs, 1, D_MODEL), *new_p, *new_s)
```

```python
import functools

import numpy as np
import jax
import jax.numpy as jnp
from jax import lax
from jax.experimental import pallas as pl
from jax.experimental.pallas import tpu as pltpu

F32 = jnp.float32
BF16 = jnp.bfloat16
I32 = jnp.int32

N_HEADS = 8
D_HEAD = 64
KV_B = 2
D_NOPE = 64
D_ROPE = 32
D_CQ = 256
D_LATENT = 128
TOPK_MAX = 256
ROPE_THETA = 10000.0
N_BUCKETS = 32
MAX_DISTANCE = 128
BRANCH_W = 512
D_MODEL = 1024
PAGE = 128
LN_EPS = 1e-5
RMS_EPS = 1e-6

LANES = 128
NEG = -1e30
LOG2E = 1.4426950408889634
INT_MIN = int(np.iinfo(np.int32).min)
VMEM_LIMIT = 56 * 1024 * 1024
FLASH_TK = 512
PV_GROUPS = 1

SRC_SPLITS = (
    ('a_q', 512), ('a_k', 64), ('a_v', 64), ('a_qidx', 512), ('a_kidx', 64), ('a_widx', 8), ('a_z', 512),
    ('b_q', 512), ('b_k', 128), ('b_v', 128), ('b_f', 8), ('b_z', 512),
    ('c_q', 256), ('c_kv', 128), ('c_krope', 32), ('c_z', 512), ('gates', 3072))
NP = 7168
OFF_Q = 0
OFF_Z = 1536
OFF_G = 3072
OFF_S = 6144
L_KROPE, L_BF, L_WIDX = 64, 96, 104


def _fox_head_perm():
    j = np.arange(BRANCH_W)
    return ((j // 128) + 4 * ((j % 128) // 64)) * 64 + (j % 64)


def _col_perm():
    src_off, o = {}, 0
    for name, n in SRC_SPLITS:
        src_off[name] = o
        o += n
    src = -np.ones((NP,), np.int64)
    scale = np.ones((NP,), np.float32)

    def put(dst, name, n, perm=None):
        idx = np.arange(n) if perm is None else perm
        src[dst:dst + n] = src_off[name] + idx

    put(0, 'a_q', 512)
    scale[0:512] = D_HEAD ** -0.5
    put(512, 'a_qidx', 512)
    put(1024, 'b_q', 512)
    scale[1024:1536] = D_HEAD ** -0.5
    put(OFF_Z, 'a_z', 512)
    put(OFF_Z + 512, 'b_z', 512, _fox_head_perm())
    put(OFF_Z + 1024, 'c_z', 512)
    put(OFF_G, 'gates', 3072)
    put(OFF_S, 'c_q', 256)
    g0 = OFF_S + 256
    put(g0, 'a_k', 64)
    put(g0 + 64, 'a_v', 64)
    g1 = g0 + 128
    put(g1, 'a_kidx', 64)
    put(g1 + L_KROPE, 'c_krope', 32)
    put(g1 + L_BF, 'b_f', 8)
    put(g1 + L_WIDX, 'a_widx', 8)
    put(g1 + 128, 'b_k', 128)
    put(g1 + 256, 'b_v', 128)
    put(g1 + 384, 'c_kv', 128)
    scale[src < 0] = 0.0
    return src, scale


def _permute_columns(w):
    src, scale = _col_perm()
    pieces, i = [], 0
    while i < NP:
        j = i + 1
        while j < NP and scale[j] == scale[i] and (
                (src[i] < 0 and src[j] < 0) or (src[i] >= 0 and src[j] == src[j - 1] + 1)):
            j += 1
        if src[i] < 0:
            pieces.append(jnp.zeros((w.shape[0], j - i), w.dtype))
        else:
            blk = w[:, int(src[i]):int(src[i]) + (j - i)]
            pieces.append(blk if scale[i] == 1.0 else blk * float(scale[i]))
        i = j
    return jnp.concatenate(pieces, axis=1).astype(BF16)


def _split3(x):
    hi = x.astype(BF16)
    r = x - hi.astype(F32)
    mid = r.astype(BF16)
    lo = (r - mid.astype(F32)).astype(BF16)
    return lo, mid, hi


def _nt(a, b):
    return lax.dot_general(a, b, (((1,), (1,)), ((), ())), preferred_element_type=F32)


def _cparams(sem):
    return pltpu.CompilerParams(dimension_semantics=sem, vmem_limit_bytes=VMEM_LIMIT)


def _mm_kernel(x_ref, w_ref, o_ref, xb_ref):
    @pl.when(pl.program_id(1) == 0)
    def _():
        xb_ref[...] = x_ref[...].astype(BF16)

    o_ref[...] = jnp.dot(xb_ref[...], w_ref[...], preferred_element_type=F32)


def _project(x, wp):
    m, k = x.shape
    n = wp.shape[1]
    tm = min(m, 1024)
    tn = 1024
    return pl.pallas_call(
        _mm_kernel,
        grid=(m // tm, n // tn),
        in_specs=[pl.BlockSpec((tm, k), lambda i, j: (i, 0)),
                  pl.BlockSpec((k, tn), lambda i, j: (0, j))],
        out_specs=pl.BlockSpec((tm, tn), lambda i, j: (i, j)),
        out_shape=jax.ShapeDtypeStruct((m, n), F32),
        scratch_shapes=[pltpu.VMEM((tm, k), BF16)],
        compiler_params=_cparams(("arbitrary", "arbitrary")),
        name="proj_mm",
    )(x, wp)


def _post_kernel(yq_ref, ys_ref, cosq_ref, sinq_ref, cosk_ref, sink_ref, bf_ref, gq_ref, gkv_ref,
                 sela_ref, selb_ref, wq2_ref, wuk_ref, psel_ref, qab_ref,
                 qa_ref, qi_ref, qb_ref, qc_ref, st_ref, kvb_ref, *, scale_c):
    yq = yq_ref[...]

    def heads(x, sel_ref, out_ref, extra_ref=None):
        full = jnp.dot(x.astype(BF16), sel_ref[...], preferred_element_type=F32)
        for h in range(N_HEADS):
            blk = full[:, h * LANES:(h + 1) * LANES]
            if extra_ref is not None:
                blk = blk + extra_ref[h:h + 1, :]
            out_ref[h] = blk.astype(BF16)

    heads(yq[:, 0:512] * LOG2E, sela_ref, qa_ref, qab_ref)
    heads(yq[:, 512:1024], sela_ref, qi_ref)
    heads(yq[:, 1024:1536] * LOG2E, selb_ref, qb_ref)

    ys = ys_ref[...]
    cq = ys[:, 0:D_CQ]
    cqn = cq * lax.rsqrt(jnp.mean(cq * cq, axis=-1, keepdims=True) + RMS_EPS) * gq_ref[...]
    quq = jnp.dot(cqn.astype(BF16), wq2_ref[...], preferred_element_type=F32)
    x1, x2 = quq[:, 512:640], quq[:, 640:768]
    cos, sin = cosq_ref[...], sinq_ref[...]
    rq = jnp.concatenate([x1 * cos - x2 * sin, x1 * sin + x2 * cos], axis=1) * scale_c
    rsel = jnp.dot(rq.astype(BF16), psel_ref[...], preferred_element_type=F32)
    qlat = jnp.dot(quq[:, 0:512].astype(BF16), wuk_ref[...], preferred_element_type=F32) * scale_c
    for h in range(N_HEADS):
        qc_ref[h, :, 0:LANES] = qlat[:, h * LANES:(h + 1) * LANES].astype(BF16)
        qc_ref[h, :, LANES:2 * LANES] = rsel[:, h * LANES:(h + 1) * LANES].astype(BF16)

    g0 = ys[:, 256:384]
    g1 = ys[:, 384:512]
    g2 = ys[:, 512:640]
    g3 = ys[:, 640:768]
    ckv = ys[:, 768:896]
    ckvn = ckv * lax.rsqrt(jnp.mean(ckv * ckv, axis=-1, keepdims=True) + RMS_EPS) * gkv_ref[...]
    lane = lax.broadcasted_iota(I32, g1.shape, 1)
    swap = jnp.where(lane < L_KROPE + 16, pltpu.roll(g1, LANES - 16, 1), pltpu.roll(g1, 16, 1))
    roped = g1 * cosk_ref[...] + swap * sink_ref[...]
    v = g1 + bf_ref[...]
    logsig = jnp.minimum(v, 0.0) - jnp.log1p(jnp.exp(-jnp.abs(v)))
    g1p = jnp.where((lane >= L_BF) & (lane < L_BF + N_HEADS), logsig, roped)
    v2 = jnp.where(lane < D_HEAD, pltpu.roll(g0, D_HEAD, 1), g0)

    st_ref[:, 0:128] = g0
    st_ref[:, 128:256] = g1p
    st_ref[:, 256:384] = g2
    st_ref[:, 384:512] = g3
    st_ref[:, 512:640] = ckvn
    kvb_ref[:, 0:128] = g0.astype(BF16)
    kvb_ref[:, 128:256] = g1p.astype(BF16)
    kvb_ref[:, 256:384] = g2.astype(BF16)
    kvb_ref[:, 384:512] = g3.astype(BF16)
    kvb_ref[:, 512:640] = ckvn.astype(BF16)
    kvb_ref[:, 640:768] = pltpu.roll(g1p, LANES - L_KROPE, 1).astype(BF16)
    kvb_ref[:, 768:896] = v2.astype(BF16)
    ones3 = jnp.where(lane < D_HEAD + 3, 1.0, 0.0)
    kvb_ref[:, 896:1024] = jnp.where(lane < D_HEAD, g0, ones3).astype(BF16)


def _post(y, tabs, consts):
    m = y.shape[0]
    tm = min(m, 256)
    row = lambda w: pl.BlockSpec((tm, w), lambda i: (i, 0))
    full = lambda a: pl.BlockSpec(a.shape, lambda i: (0,) * a.ndim)
    cosq, sinq, cosk, sink = tabs
    bf_row, gq, gkv, sela, selb, wq2, wuk, psel, qab = consts
    hq =lambda w: pl.BlockSpec((N_HEADS, tm, w), lambda i: (0, i, 0))
    kern = functools.partial(_post_kernel, scale_c=float((D_NOPE + D_ROPE) ** -0.5 * LOG2E))
    return pl.pallas_call(
        kern,
        grid=(m // tm,),
        in_specs=[pl.BlockSpec((tm, 1536), lambda i: (i, 0)),
                  pl.BlockSpec((tm, 1024), lambda i: (i, OFF_S // 1024)),
                  row(128), row(128), row(128), row(128),
                  full(bf_row), full(gq), full(gkv), full(sela), full(selb), full(wq2), full(wuk), full(psel),
                  full(qab)],
        out_specs=[hq(128), hq(128), hq(128), hq(256), row(640), row(1024)],
        out_shape=[jax.ShapeDtypeStruct((N_HEADS, m, 128), BF16),
                   jax.ShapeDtypeStruct((N_HEADS, m, 128), BF16),
                   jax.ShapeDtypeStruct((N_HEADS, m, 128), BF16),
                   jax.ShapeDtypeStruct((N_HEADS, m, 256), BF16),
                   jax.ShapeDtypeStruct((m, 640), F32),
                   jax.ShapeDtypeStruct((m, 1024), BF16)],
        compiler_params=_cparams(("arbitrary",)),
        name="post_proj",
    )(y, y, cosq, sinq, cosk, sink, bf_row, gq, gkv, sela, selb, wq2, wuk, psel, qab)


def _page_prefix_kernel(x_ref, m1_ref, m2_ref, o_ref):
    parts = _split3(x_ref[...])
    w = None
    t = None
    for p in parts:
        dw = jnp.dot(p, m1_ref[...], preferred_element_type=F32)
        dt = jnp.dot(p, m2_ref[...], preferred_element_type=F32)
        w = dw if w is None else w + dw
        t = dt if t is None else t + dt
    o_ref[:, 0:1024] = w
    o_ref[:, 1024:2048] = t


def _page_prefix(x, m1, m2):
    r = x.shape[0]
    tr = min(r, 512)
    assert r % tr == 0
    return pl.pallas_call(
        _page_prefix_kernel,
        grid=(r // tr,),
        in_specs=[pl.BlockSpec((tr, 1024), lambda i: (i, 0)),
                  pl.BlockSpec((1024, 1024), lambda i: (0, 0)),
                  pl.BlockSpec((1024, 1024), lambda i: (0, 0))],
        out_specs=pl.BlockSpec((tr, 2048), lambda i: (i, 0)),
        out_shape=jax.ShapeDtypeStruct((r, 2048), F32),
        compiler_params=_cparams(("arbitrary",)),
        name="page_prefix",
    )(x, m1, m2)


def _blk_prefix_kernel(wt_ref, ltri_ref, o_ref):
    w = wt_ref[:, 0:1024]
    acc = None
    for p in _split3(wt_ref[:, 1024:2048]):
        d = jnp.dot(ltri_ref[...], p, preferred_element_type=F32)
        acc = d if acc is None else acc + d
    o_ref[...] = -(w + acc) * LOG2E


def _blk_prefix(wt, ltri):
    nb = wt.shape[0]
    return pl.pallas_call(
        _blk_prefix_kernel,
        grid=(1,),
        in_specs=[pl.BlockSpec((nb, 2048), lambda i: (0, 0)), pl.BlockSpec((nb, nb), lambda i: (0, 0))],
        out_specs=pl.BlockSpec((nb, 1024), lambda i: (0, 0)),
        out_shape=jax.ShapeDtypeStruct((nb, 1024), F32),
        compiler_params=_cparams(("arbitrary",)),
        name="blk_prefix",
    )(wt, ltri)


def _softmax_update(s_list, v, m_ref, l_ref, acc_ref, tq):
    tk = s_list[0].shape[1]
    hg = N_HEADS // PV_GROUPS
    for g in range(PV_GROUPS):
        ps, alphas = [], []
        for h in range(g * hg, (g + 1) * hg):
            rows = slice(h * tq, (h + 1) * tq)
            s = s_list[h]
            m_prev = m_ref[rows, :]
            m_new = jnp.maximum(m_prev, jnp.max(s, axis=1, keepdims=True))
            alpha = jnp.exp2(m_prev - m_new)
            p = jnp.exp2(s - jnp.concatenate([m_new] * (tk // LANES), axis=1))
            l_ref[rows, :] = alpha * l_ref[rows, :] + jnp.sum(p, axis=1, keepdims=True)
            m_ref[rows, :] = m_new
            ps.append(p.astype(BF16))
            alphas.append(alpha)
        grows = slice(g * hg * tq, (g + 1) * hg * tq)
        pv = jnp.dot(jnp.concatenate(ps, axis=0), v, preferred_element_type=F32)
        acc_ref[grows, :] = jnp.concatenate(alphas, axis=0) * acc_ref[grows, :] + pv


def _init_state(m_ref, l_ref, acc_ref):
    m_ref[...] = jnp.full(m_ref.shape, NEG, F32)
    l_ref[...] = jnp.zeros(l_ref.shape, F32)
    acc_ref[...] = jnp.zeros(acc_ref.shape, F32)


def _float_key(x):
    bits = pltpu.bitcast(x, I32)
    return bits ^ ((bits >> 31) & 0x7FFFFFFF)


def _kth_largest_key(count_ge, k, shape, bits=32):
    zero = jnp.zeros(shape, I32)
    prefix = jnp.where(count_ge(zero) >= k, zero, jnp.full(shape, -(1 << (bits - 1)), I32))

    def bit_body(it, prefix):
        cand = prefix + jnp.left_shift(jnp.int32(1), bits - 2 - it)
        return jnp.where(count_ge(cand) >= k, cand, prefix)

    return lax.fori_loop(0, bits - 1, bit_body, prefix)


def _dsa_prompt_kernel(qi_ref, qa_ref, w_ref, ka_ref, k1_ref, v2_ref, btile_ref, o_ref,
                       key_scr, m_scr, l_scr, acc_scr, *, tq, tk, topk):
    i = pl.program_id(0)
    t0 = i * tq
    nsl = tk // LANES
    qi = qi_ref[...].reshape(N_HEADS * tq, LANES)
    qa = qa_ref[...].reshape(N_HEADS * tq, LANES)
    w = w_ref[...]
    wb = [jnp.broadcast_to(w[:, L_WIDX + h:L_WIDX + h + 1], (tq, tk)) for h in range(N_HEADS)]
    qpos = t0 + lax.broadcasted_iota(I32, (tq, tk), 0)
    lane_k = lax.broadcasted_iota(I32, (tq, tk), 1)
    nc = (t0 + tq + tk - 1) // tk

    def score_body(c, carry):
        k0 = pl.multiple_of(c * tk, tk)
        y = _nt(qi, k1_ref[pl.ds(k0, tk), :])
        acc = wb[0] * jnp.maximum(y[0:tq], 0.0)
        for h in range(1, N_HEADS):
            acc = acc + wb[h] * jnp.maximum(y[h * tq:(h + 1) * tq], 0.0)
        acc = jnp.where(k0 + lane_k <= qpos, acc, -jnp.inf)
        key = _float_key(acc)
        for j in range(nsl):
            key_scr[c * nsl + j] = key[:, j * LANES:(j + 1) * LANES]
        return carry

    lax.fori_loop(0, nc, score_body, 0)

    def count_ge(cand):
        def body(c, cnt):
            for j in range(nsl):
                cnt = cnt + (key_scr[c * nsl + j] >= cand).astype(I32)
            return cnt

        cnt = lax.fori_loop(0, nc, body, jnp.zeros((tq, LANES), I32))
        return jnp.sum(cnt.astype(F32), axis=1, keepdims=True)

    thr = _kth_largest_key(count_ge, float(topk), (tq, LANES))

    _init_state(m_scr, l_scr, acc_scr)
    far_end = jnp.maximum(t0 - tq, 0)
    nf = (far_end + tk - 1) // tk
    thr_k = jnp.concatenate([thr] * nsl, axis=1)

    def far_step(c, clip):
        k0 = pl.multiple_of(c * tk, tk)
        s = _nt(qa, ka_ref[pl.ds(k0, tk), :])
        keyc = jnp.concatenate([key_scr[c * nsl + j] for j in range(nsl)], axis=1)
        if clip:
            keyc = jnp.where(k0 + lane_k < far_end, keyc, INT_MIN)
        drop = jnp.where(keyc >= thr_k, 0.0, NEG)
        s_list = [s[h * tq:(h + 1) * tq] + drop for h in range(N_HEADS)]
        _softmax_update(s_list, v2_ref[pl.ds(k0, tk), :], m_scr, l_scr, acc_scr, tq)

    def far_body(c, carry):
        far_step(c, False)
        return carry

    lax.fori_loop(0, nf - 1, far_body, 0)

    @pl.when(nf > 0)
    def _():
        far_step(nf - 1, True)

    ns = pl.multiple_of(far_end, tq)
    sl = far_end // LANES
    s = _nt(qa, ka_ref[pl.ds(ns, 2 * tq), :])
    keyw = jnp.concatenate([key_scr[sl], key_scr[sl + 1]], axis=1)
    kposw = ns + lax.broadcasted_iota(I32, (tq, 2 * tq), 1)
    qposw = t0 + lax.broadcasted_iota(I32, (tq, 2 * tq), 0)
    selw = (keyw >= jnp.concatenate([thr, thr], axis=1)) & (kposw <= qposw)
    first = jnp.where(i == 0, 1, 0)
    s_list = []
    for h in range(N_HEADS):
        bias = jnp.concatenate([btile_ref[first, h], btile_ref[1, h]], axis=1)
        s_list.append(jnp.where(selw, s[h * tq:(h + 1) * tq] + bias, NEG))
    _softmax_update(s_list, v2_ref[pl.ds(ns, 2 * tq), :], m_scr, l_scr, acc_scr, tq)

    lane = lax.broadcasted_iota(I32, (tq, LANES), 1)
    for c in range(N_HEADS // 2):
        r0 = slice(2 * c * tq, (2 * c + 1) * tq)
        r1 = slice((2 * c + 1) * tq, (2 * c + 2) * tq)
        o_ref[:, c * LANES:(c + 1) * LANES] = jnp.where(
            lane < D_HEAD, acc_scr[r0, :] / l_scr[r0, :], acc_scr[r1, :] / l_scr[r1, :])


def _dsa_prompt(qi, qa, st, kvb, btile, topk):
    s = st.shape[0]
    tq, tk = 128, min(512, s)
    assert s % tk == 0 and s >= 2 * tq
    hq = pl.BlockSpec((N_HEADS, tq, 128), lambda i: (0, i, 0))
    col = lambda j: pl.BlockSpec((s, 128), lambda i: (0, j))
    kern = functools.partial(_dsa_prompt_kernel, tq=tq, tk=tk, topk=topk)
    return pl.pallas_call(
        kern,
        grid=(s // tq,),
        in_specs=[hq, hq,
                  pl.BlockSpec((tq, 128), lambda i: (i, 1)),
                  col(7), col(1), col(6),
                  pl.BlockSpec(btile.shape, lambda i: (0, 0, 0, 0))],
        out_specs=pl.BlockSpec((tq, BRANCH_W), lambda i: (i, 0)),
        out_shape=jax.ShapeDtypeStruct((s, BRANCH_W), F32),
        scratch_shapes=[pltpu.VMEM((s // LANES, tq, LANES), I32),
                        pltpu.VMEM((N_HEADS * tq, LANES), F32),
                        pltpu.VMEM((N_HEADS * tq, LANES), F32),
                        pltpu.VMEM((N_HEADS * tq, LANES), F32)],
        compiler_params=_cparams(("arbitrary",)),
        name="dsa_prompt",
    )(qi, qa, st, kvb, kvb, kvb, btile)


def _flash_prompt_kernel(q_ref, kv_ref, aux_ref, o_ref, m_scr, l_scr, acc_scr, *, tq, tk, mode):
    i = pl.program_id(0)
    t0 = i * tq
    dq = q_ref.shape[2]
    q = q_ref[...].reshape(N_HEADS * tq, dq)
    nfull = t0 // tk
    _init_state(m_scr, l_scr, acc_scr)

    def kv(k0):
        if mode == "fox":
            return kv_ref[pl.ds(k0, tk), 0:LANES], kv_ref[pl.ds(k0, tk), LANES:2 * LANES]
        k = kv_ref[pl.ds(k0, tk), :]
        return k, k[:, 0:LANES]

    def logits(c, k):
        s = _nt(q, k)
        out = []
        for h in range(N_HEADS):
            sh = s[h * tq:(h + 1) * tq]
            if mode == "fox":
                sh = sh + aux_ref[c, h:h + 1, :]
            out.append(sh)
        return out

    def full_body(c, carry):
        k, v = kv(pl.multiple_of(c * tk, tk))
        _softmax_update(logits(c, k), v, m_scr, l_scr, acc_scr, tq)
        return carry

    lax.fori_loop(0, nfull, full_body, 0)

    k0 = pl.multiple_of(nfull * tk, tk)
    k, v = kv(k0)
    causal = (k0 + lax.broadcasted_iota(I32, (tq, tk), 1)) <= (t0 + lax.broadcasted_iota(I32, (tq, tk), 0))
    s_list = [jnp.where(causal, sh, NEG) for sh in logits(nfull, k)]
    _softmax_update(s_list, v, m_scr, l_scr, acc_scr, tq)

    if mode == "fox":
        lane = lax.broadcasted_iota(I32, (tq, LANES), 1)
        for c in range(N_HEADS // 2):
            r0 = slice(c * tq, (c + 1) * tq)
            r1 = slice((c + 4) * tq, (c + 5) * tq)
            o_ref[:, c * LANES:(c + 1) * LANES] = jnp.where(
                lane < D_HEAD, acc_scr[r0, :] / l_scr[r0, :], acc_scr[r1, :] / l_scr[r1, :])
    else:
        out = None
        for h in range(N_HEADS):
            rows = slice(h * tq, (h + 1) * tq)
            olat = (acc_scr[rows, :] / l_scr[rows, :]).astype(BF16)
            d = jnp.dot(olat, aux_ref[h], preferred_element_type=F32)
            out = d if out is None else out + d
        o_ref[...] = out


def _flash_prompt(q, kvb, aux, mode):
    s = kvb.shape[0]
    tq, tk = 128, min(FLASH_TK, s)
    dq = q.shape[2]
    kern = functools.partial(_flash_prompt_kernel, tq=tq, tk=tk, mode=mode)
    return pl.pallas_call(
        kern,
        grid=(s // tq,),
        in_specs=[pl.BlockSpec((N_HEADS, tq, dq), lambda i: (0, i, 0)),
                  pl.BlockSpec((s, 256), lambda i: (0, 1 if mode == "fox" else 2)),
                  pl.BlockSpec(aux.shape, lambda i: (0, 0, 0))],
        out_specs=pl.BlockSpec((tq, BRANCH_W), lambda i: (i, 0)),
        out_shape=jax.ShapeDtypeStruct((s, BRANCH_W), F32),
        scratch_shapes=[pltpu.VMEM((N_HEADS * tq, LANES), F32),
                        pltpu.VMEM((N_HEADS * tq, LANES), F32),
                        pltpu.VMEM((N_HEADS * tq, LANES), F32)],
        compiler_params=_cparams(("arbitrary",)),
        name=mode + "_prompt",
    )(q, kvb, aux)


def _merge_kernel(oa_ref, ob_ref, oc_ref, z_ref, g_ref, x_ref, wbr_ref, wout_ref, lng_ref, lnb_ref, o_ref,
                  *, alpha):
    mixed = None
    for n, o in enumerate((oa_ref, ob_ref, oc_ref)):
        zn = z_ref[:, n * BRANCH_W:(n + 1) * BRANCH_W]
        u = o[...] * (zn * jax.nn.sigmoid(zn))
        y = jnp.dot(u.astype(BF16), wbr_ref[n], preferred_element_type=F32)
        t = jax.nn.sigmoid(g_ref[:, n * D_MODEL:(n + 1) * D_MODEL]) * y
        mixed = t if mixed is None else mixed + t
    out = jnp.dot(mixed.astype(BF16), wout_ref[...], preferred_element_type=F32)
    hres = alpha * x_ref[...] + out
    mu = jnp.mean(hres, axis=-1, keepdims=True)
    d = hres - mu
    var = jnp.mean(d * d, axis=-1, keepdims=True)
    o_ref[...] = d * lax.rsqrt(var + LN_EPS) * lng_ref[...] + lnb_ref[...]


def _merge(oa, ob, oc, y, x, wbr, wout, lng, lnb, alpha):
    m = x.shape[0]
    tm = min(m, 256)
    row = lambda w: pl.BlockSpec((tm, w), lambda i: (i, 0))
    full = lambda a: pl.BlockSpec(a.shape, lambda i: (0,) * a.ndim)
    return pl.pallas_call(
        functools.partial(_merge_kernel, alpha=alpha),
        grid=(m // tm,),
        in_specs=[row(BRANCH_W), row(BRANCH_W), row(BRANCH_W),
                  pl.BlockSpec((tm, 1536), lambda i: (i, OFF_Z // 1536)),
                  pl.BlockSpec((tm, 3072), lambda i: (i, OFF_G // 3072)),
                  row(D_MODEL), full(wbr), full(wout), full(lng), full(lnb)],
        out_specs=row(D_MODEL),
        out_shape=jax.ShapeDtypeStruct((m, D_MODEL), F32),
        compiler_params=_cparams(("arbitrary",)),
        name="merge",
    )(oa, ob, oc, y, y, x, wbr, wout, lng, lnb)


def _np_consts():
    sela = np.zeros((512, 1024), np.float32)
    selb = np.zeros((512, 1024), np.float32)
    for h in range(N_HEADS):
        for d in range(D_HEAD):
            sela[h * 64 + d, h * 128 + d] = 1.0
            selb[h * 64 + d, h * 128 + (0 if h < 4 else 64) + d] = 1.0
    psel = np.zeros((256, 1024), np.float32)
    for h in range(N_HEADS):
        for j in range(16):
            psel[h * 16 + j, h * 128 + j] = 1.0
            psel[128 + h * 16 + j, h * 128 + 16 + j] = 1.0
    i = np.arange(1024)
    r, h = i // 8, i % 8
    hp, rp = i // 128, i % 128
    m2 = (h[:, None] == hp[None, :])
    m1 = m2 & (r[:, None] <= rp[None, :])
    return sela, selb, psel, m1.astype(np.float32), m2.astype(np.float32)


def _t5_bucket(dist):
    max_exact = N_BUCKETS // 2
    d = jnp.maximum(dist, 1).astype(F32)
    large = max_exact + (jnp.log(d / max_exact) / np.log(MAX_DISTANCE / max_exact)
                         * (N_BUCKETS - max_exact)).astype(I32)
    large = jnp.minimum(large, N_BUCKETS - 1)
    return jnp.where(dist < max_exact, dist, large)


def _rope_tables(pos):
    half = D_ROPE // 2
    freqs = ROPE_THETA ** (-jnp.arange(half, dtype=F32) / half)
    ang = pos.astype(F32)[:, None] * freqs
    cos, sin = jnp.cos(ang), jnp.sin(ang)
    m = pos.shape[0]
    cosq, sinq = jnp.tile(cos, (1, N_HEADS)), jnp.tile(sin, (1, N_HEADS))
    tail = LANES - L_KROPE - D_ROPE
    cosk = jnp.concatenate([jnp.ones((m, L_KROPE), F32), cos, cos, jnp.ones((m, tail), F32)], axis=1)
    sink = jnp.concatenate([jnp.zeros((m, L_KROPE), F32), -sin, sin, jnp.zeros((m, tail), F32)], axis=1)
    return cosq, sinq, cosk, sink


def _prep_layer(l, qab, w_in, b_forget, w_c_uq, g_c_q, g_c_kv, w_c_ukv, w_branch, w_out, ln_g, ln_b):
    sela, selb, psel, _, _ = _np_consts()
    wp = _permute_columns(w_in[l])
    bf_row = jnp.zeros((1, LANES), F32).at[0, L_BF:L_BF + N_HEADS].set(b_forget[l])
    wq = w_c_uq[l]
    wq2 = jnp.concatenate([wq[:, :, :D_NOPE].reshape(D_CQ, 512),
                           wq[:, :, D_NOPE:D_NOPE + 16].reshape(D_CQ, 128),
                           wq[:, :, D_NOPE + 16:].reshape(D_CQ, 128)], axis=1).astype(BF16)
    w_uk, w_uv = w_c_ukv[l][..., :D_NOPE], w_c_ukv[l][..., D_NOPE:]
    eye = jnp.eye(N_HEADS, dtype=F32)
    wukbd = jnp.einsum('chn,hg->hngc', w_uk, eye).reshape(512, 1024).astype(BF16)
    wuvpad = jnp.einsum('chv,hg->hcgv', w_uv, eye).reshape(N_HEADS, D_LATENT, 512).astype(BF16)
    wbr = w_branch[l].at[1].set(w_branch[l][1][jnp.asarray(_fox_head_perm())]).astype(BF16)
    return dict(
        wp=wp, wuvpad=wuvpad, wbr=wbr, wout=w_out[l].astype(BF16), lng=ln_g[l][None], lnb=ln_b[l][None],
        post=(bf_row, g_c_q[l][None], g_c_kv[l][None], jnp.asarray(sela, BF16), jnp.asarray(selb, BF16),
              wq2, wukbd, jnp.asarray(psel, BF16), qab))


def _bias_tables(t5_table):
    r = jnp.arange(PAGE)[:, None]
    c = jnp.arange(PAGE)[None, :]
    dist = jnp.stack([PAGE + r - c, jnp.maximum(r - c, 0)])
    btile = jnp.transpose(t5_table[_t5_bucket(dist)], (0, 3, 1, 2))
    bfar = t5_table[_t5_bucket(jnp.asarray(2 * PAGE))].astype(F32) * LOG2E
    lo, mid, hi = _split3(bfar)
    qab = jnp.zeros((N_HEADS, LANES), F32).at[:, D_HEAD:D_HEAD + 3].set(
        jnp.stack([hi, mid, lo], axis=1).astype(F32))
    btile = btile.astype(F32) * LOG2E - bfar[None, :, None, None]
    return btile, bfar, qab


def _prompt_layer(xp, prep, tabs, btile, topk, alpha, m1, m2):
    s = xp.shape[0]
    y = _project(xp, prep['wp'])
    qa, qi, qb, qc, st, kvb = _post(y, tabs, prep['post'])
    nb = s // PAGE
    wt = _page_prefix(st[:, 128 + L_BF:128 + L_BF + N_HEADS].reshape(nb, 1024), m1, m2)
    ltri = jnp.asarray(np.tril(np.ones((nb, nb), np.float32), -1), BF16)
    nf = _blk_prefix(wt, ltri)
    tk = min(FLASH_TK, s)
    nf = jnp.transpose(nf.reshape(nb, N_HEADS, PAGE), (1, 0, 2)).reshape(N_HEADS, s // tk, tk)
    nf = jnp.transpose(nf, (1, 0, 2))
    o_a = _dsa_prompt(qi, qa, st, kvb, btile, topk)
    o_b = _flash_prompt(qb, kvb, nf, "fox")
    o_c = _flash_prompt(qc, kvb, prep['wuvpad'], "mla")
    xp = _merge(o_a, o_b, o_c, y, xp, prep['wbr'], prep['wout'], prep['lng'], prep['lnb'], alpha)
    return xp, st


def _state_rows(st, lead):
    f = lambda a, *tail: a.reshape(lead + tail)
    return (f(st[:, 0:64], 64), f(st[:, 64:128], 64), f(st[:, 128:192], 64),
            f(st[:, 256:384], KV_B, D_HEAD), f(st[:, 384:512], KV_B, D_HEAD),
            f(st[:, 128 + L_BF:128 + L_BF + N_HEADS], N_HEADS),
            f(st[:, 512:640], D_LATENT), f(st[:, 128 + L_KROPE:128 + L_KROPE + D_ROPE], D_ROPE))


def _page_pipeline(pt_ref, caches, bufs, sems, layer, pps):
    b, c = pl.program_id(0), pl.program_id(1)
    nb, nch = pl.num_programs(0), pl.num_programs(1)
    n = b * nch + c
    slot = lax.rem(n, 2)

    def copies(bb, cc, sl):
        out = []
        for cache, buf, sem in zip(caches, bufs, sems):
            for p in range(pps):
                page = pt_ref[bb, cc * pps + p]
                out.append(pltpu.make_async_copy(cache.at[layer, page], buf.at[sl, p], sem.at[sl]))
        return out

    @pl.when(n == 0)
    def _():
        for cp in copies(b, c, slot):
            cp.start()

    @pl.when(n + 1 < nb * nch)
    def _():
        wrap = c + 1 == nch
        for cp in copies(jnp.where(wrap, b + 1, b), jnp.where(wrap, 0, c + 1), 1 - slot):
            cp.start()

    for cp in copies(b, c, slot):
        cp.wait()
    return slot


def _page_scratch(page_shapes, pps):
    return ([pltpu.VMEM((2, pps) + tuple(ps), F32) for ps in page_shapes]
            + [pltpu.SemaphoreType.DMA((2,)) for _ in page_shapes])


_ANY = pl.BlockSpec(memory_space=pl.ANY)


def _pages_t(cache):
    nd = cache.ndim
    return jnp.transpose(cache, (0, 1) + tuple(range(3, nd)) + (2,))


def _seq_spec(*shape):
    nd = len(shape)
    return pl.BlockSpec((None,) + shape, lambda b, c, pt: (b,) + (0,) * nd)


def _tile_lanes(x, n):
    return jnp.concatenate([x] * n, axis=1) if n > 1 else x


def _dec_update(s, pv_fn, m_ref, l_ref, acc_ref):
    dv = acc_ref.shape[1]
    m_prev = m_ref[...]
    m_new = jnp.maximum(m_prev, jnp.max(s, axis=1, keepdims=True))
    alpha = jnp.exp2(m_prev - m_new)
    p = jnp.exp2(s - _tile_lanes(m_new, s.shape[1] // LANES))
    l_ref[...] = alpha * l_ref[...] + jnp.sum(p, axis=1, keepdims=True)
    m_ref[...] = m_new
    acc_ref[...] = alpha[:, 0:dv] * acc_ref[...] + pv_fn(p.astype(BF16))


def _dec_self(s_self, v_self, m_ref, l_ref, acc_ref):
    dv = acc_ref.shape[1]
    m_prev = m_ref[...]
    m_new = jnp.maximum(m_prev, s_self)
    alpha = jnp.exp2(m_prev - m_new)
    p = jnp.exp2(s_self - m_new)
    l_ref[...] = alpha * l_ref[...] + p
    m_ref[...] = m_new
    acc_ref[...] = alpha[:, 0:dv] * acc_ref[...] + p[:, 0:dv] * v_self


def _idx_sample_kernel(pt_ref, q_ref, w_ref, g1_ref, kidx_hbm, sc_ref, self_ref, kbuf, ksem, *, pps, layer):
    slot = _page_pipeline(pt_ref, [kidx_hbm], [kbuf], [ksem], layer, pps)
    q = q_ref[...]
    w = w_ref[...]
    kt = jnp.concatenate([kbuf[slot, p] for p in range(pps)], axis=1).astype(BF16)
    y = jnp.dot(q[:, 0:D_HEAD], kt, preferred_element_type=F32)
    sc_ref[...] = jnp.sum(jnp.maximum(y, 0.0) * _tile_lanes(w, pps), axis=0, keepdims=True)

    @pl.when(pl.program_id(1) == 0)
    def _():
        ys = jnp.sum(q.astype(F32) * g1_ref[...], axis=1, keepdims=True)
        val = jnp.sum(jnp.maximum(ys, 0.0) * w[:, 0:1], axis=0, keepdims=True)
        lane = lax.broadcasted_iota(I32, (1, LANES), 1)
        self_ref[...] = jnp.where(lane == 0, val, -jnp.inf)


def _idx_sample(page_table, layer, qi_t, w8, g1new, cache_kidx, pps):
    b, npages = page_table.shape
    nch = npages // pps
    grid_spec = pltpu.PrefetchScalarGridSpec(
        num_scalar_prefetch=1, grid=(b, nch),
        in_specs=[_seq_spec(N_HEADS, LANES), _seq_spec(N_HEADS, LANES), _seq_spec(1, LANES), _ANY],
        out_specs=[pl.BlockSpec((None, 1, PAGE * pps), lambda b, c, pt: (b, 0, c)), _seq_spec(1, LANES)],
        scratch_shapes=_page_scratch([(D_HEAD, PAGE)], pps))
    return pl.pallas_call(
        functools.partial(_idx_sample_kernel, pps=pps, layer=layer), grid_spec=grid_spec,
        out_shape=[jax.ShapeDtypeStruct((b, 1, npages * PAGE), F32), jax.ShapeDtypeStruct((b, 1, LANES), F32)],
        compiler_params=_cparams(("arbitrary", "arbitrary")), name="idx_sample",
    )(page_table, qi_t, w8, g1new, cache_kidx)


def _thresh_kernel(sc_ref, o_ref, key_scr, *, topk):
    rows = sc_ref.shape[0]
    nsl = sc_ref.shape[1] // LANES
    for j in range(nsl):
        key_scr[j] = _float_key(sc_ref[:, j * LANES:(j + 1) * LANES])

    def count_ge(cand):
        cnt = lax.fori_loop(0, nsl, lambda j, cnt: cnt + (key_scr[j] >= cand).astype(I32),
                            jnp.zeros((rows, LANES), I32))
        return jnp.sum(cnt.astype(F32), axis=1, keepdims=True)

    thr = _kth_largest_key(count_ge, float(topk), (rows, LANES))
    bits = thr ^ ((thr >> 31) & 0x7FFFFFFF)
    o_ref[...] = jnp.where(thr == INT_MIN, -jnp.inf, pltpu.bitcast(bits, F32))


def _thresh(sc, topk):
    rows, n = sc.shape
    return pl.pallas_call(
        functools.partial(_thresh_kernel, topk=topk), grid=(1,),
        in_specs=[pl.BlockSpec((rows, n), lambda i: (0, 0))],
        out_specs=pl.BlockSpec((rows, LANES), lambda i: (0, 0)),
        out_shape=jax.ShapeDtypeStruct((rows, LANES), F32),
        scratch_shapes=[pltpu.VMEM((n // LANES, rows, LANES), I32)],
        compiler_params=_cparams(("arbitrary",)), name="thresh_sample",
    )(sc)


def _dsa_sample_kernel(pt_ref, q_ref, sc_ref, self_ref, thr_ref, g0_ref, bfar_ref, blast_ref, bself_ref,
                       k_hbm, v_hbm, o_ref, m_scr, l_scr, acc_scr, kbuf, vbuf, ksem, vsem, *, pps, layer):
    slot = _page_pipeline(pt_ref, [k_hbm, v_hbm], [kbuf, vbuf], [ksem, vsem], layer, pps)
    c = pl.program_id(1)
    last = c == pl.num_programs(1) - 1

    @pl.when(c == 0)
    def _():
        _init_state(m_scr, l_scr, acc_scr)

    q = q_ref[...]
    kt = jnp.concatenate([kbuf[slot, p] for p in range(pps)], axis=1).astype(BF16)
    vt = jnp.concatenate([vbuf[slot, p] for p in range(pps)], axis=1).astype(BF16)
    thr = thr_ref[...]
    sel = sc_ref[...] >= _tile_lanes(thr, pps)
    bfar = bfar_ref[...]
    bias = jnp.concatenate([bfar] * (pps - 1) + [jnp.where(last, blast_ref[...], bfar)], axis=1)
    s = jnp.where(sel, jnp.dot(q[:, 0:D_HEAD], kt, preferred_element_type=F32) + bias, NEG)
    _dec_update(s, lambda p: _nt(p, vt), m_scr, l_scr, acc_scr)

    @pl.when(last)
    def _():
        g0 = g0_ref[...]
        s_self = (jnp.sum(q[:, 0:D_HEAD].astype(F32) * g0[:, 0:D_HEAD], axis=1, keepdims=True)
                  + bself_ref[...])
        s_self = jnp.where(self_ref[:, 0:1] >= thr[:, 0:1], s_self, NEG)
        _dec_self(s_self, g0[:, D_HEAD:2 * D_HEAD], m_scr, l_scr, acc_scr)
        o_ref[...] = acc_scr[...] / l_scr[:, 0:D_HEAD]


def _dsa_sample(page_table, layer, qa_t, sc, selfsc, thr, g0new, bfar8, blast, bself8, cache_k, cache_v, pps):
    b, npages = page_table.shape
    full = lambda a: pl.BlockSpec(a.shape, lambda b, c, pt: (0,) * a.ndim)
    grid_spec = pltpu.PrefetchScalarGridSpec(
        num_scalar_prefetch=1, grid=(b, npages // pps),
        in_specs=[_seq_spec(N_HEADS, LANES), pl.BlockSpec((None, 1, PAGE * pps), lambda b, c, pt: (b, 0, c)),
                  _seq_spec(1, LANES), _seq_spec(1, LANES), _seq_spec(1, LANES), full(bfar8), full(blast), full(bself8),
                  _ANY, _ANY],
        out_specs=_seq_spec(N_HEADS, D_HEAD),
        scratch_shapes=[pltpu.VMEM((N_HEADS, LANES), F32), pltpu.VMEM((N_HEADS, LANES), F32),
                        pltpu.VMEM((N_HEADS, D_HEAD), F32)] + _page_scratch([(D_HEAD, PAGE)] * 2, pps))
    return pl.pallas_call(
        functools.partial(_dsa_sample_kernel, pps=pps, layer=layer), grid_spec=grid_spec,
        out_shape=jax.ShapeDtypeStruct((b, N_HEADS, D_HEAD), F32),
        compiler_params=_cparams(("arbitrary", "arbitrary")), name="dsa_sample",
    )(page_table, qa_t, sc, selfsc, thr, g0new, bfar8, blast, bself8, cache_k, cache_v)


def _fox_sample_kernel(pt_ref, q_ref, st_ref, lf_ref, utri_ref, ones_ref, k_hbm, v_hbm, f_hbm,
                       o_ref, m_scr, l_scr, acc_scr, carry_scr, kbuf, vbuf, fbuf, ksem, vsem, fsem, *, pps, layer):
    slot = _page_pipeline(pt_ref, [k_hbm, v_hbm, f_hbm], [kbuf, vbuf, fbuf], [ksem, vsem, fsem], layer, pps)
    kp = [kbuf.at[slot, p] for p in range(pps)]
    vp = [vbuf.at[slot, p] for p in range(pps)]
    fp = [fbuf.at[slot, p] for p in range(pps)]
    c = pl.program_id(1)

    @pl.when(c == 0)
    def _():
        _init_state(m_scr, l_scr, acc_scr)
        carry_scr[...] = jnp.zeros(carry_scr.shape, F32)

    q = q_ref[...]
    q0, q1 = q[:, 0:D_HEAD], q[:, D_HEAD:2 * D_HEAD]
    kv = lambda pages, g: jnp.concatenate([p[g] for p in pages], axis=1).astype(BF16)
    s = (jnp.dot(q0, kv(kp, 0), preferred_element_type=F32)
         + jnp.dot(q1, kv(kp, 1), preferred_element_type=F32))
    parts = _split3(jnp.concatenate([p[...] for p in fp], axis=0))
    wsum = tsum = None
    for part in parts:
        dw = jnp.dot(part, utri_ref[...], preferred_element_type=F32)
        dt = jnp.dot(part, ones_ref[...], preferred_element_type=F32)
        wsum = dw if wsum is None else wsum + dw
        tsum = dt if tsum is None else tsum + dt
    po = carry_scr[...]
    biases = []
    for p in range(pps):
        biases.append(wsum[p * N_HEADS:(p + 1) * N_HEADS, :] + po)
        po = po + tsum[p * N_HEADS:(p + 1) * N_HEADS, :]
    carry_scr[...] = po
    s = s - jnp.concatenate(biases, axis=1) * LOG2E
    v0, v1 = kv(vp, 0), kv(vp, 1)
    row = lax.broadcasted_iota(I32, (N_HEADS, D_HEAD), 0)
    lower = row < N_HEADS // KV_B
    _dec_update(s, lambda p: jnp.where(lower, _nt(p, v0), _nt(p, v1)), m_scr, l_scr, acc_scr)

    @pl.when(c == pl.num_programs(1) - 1)
    def _():
        k_new, v_new = st_ref[:, 256:384], st_ref[:, 384:512]
        s_self = jnp.sum(q.astype(F32) * k_new, axis=1, keepdims=True) - (po + lf_ref[...]) * LOG2E
        v_self = jnp.where(lower, jnp.broadcast_to(v_new[:, 0:D_HEAD], (N_HEADS, D_HEAD)),
                           jnp.broadcast_to(v_new[:, D_HEAD:2 * D_HEAD], (N_HEADS, D_HEAD)))
        _dec_self(s_self, v_self, m_scr, l_scr, acc_scr)
        o_ref[...] = acc_scr[...] / l_scr[:, 0:D_HEAD]


def _fox_sample(page_table, layer, qb_t, st3, lf8, cache_k, cache_v, cache_f, pps):
    b, npages = page_table.shape
    utri = jnp.asarray(np.triu(np.ones((PAGE, PAGE), np.float32)), BF16)
    ones = jnp.ones((PAGE, PAGE), BF16)
    full = lambda a: pl.BlockSpec(a.shape, lambda b, c, pt: (0,) * a.ndim)
    grid_spec = pltpu.PrefetchScalarGridSpec(
        num_scalar_prefetch=1, grid=(b, npages // pps),
        in_specs=[_seq_spec(N_HEADS, LANES), _seq_spec(1, 640), _seq_spec(N_HEADS, LANES), full(utri), full(ones),
                  _ANY, _ANY, _ANY],
        out_specs=_seq_spec(N_HEADS, D_HEAD),
        scratch_shapes=[pltpu.VMEM((N_HEADS, LANES), F32), pltpu.VMEM((N_HEADS, LANES), F32),
                        pltpu.VMEM((N_HEADS, D_HEAD), F32), pltpu.VMEM((N_HEADS, LANES), F32)]
        + _page_scratch([(KV_B, D_HEAD, PAGE), (KV_B, D_HEAD, PAGE), (N_HEADS, PAGE)], pps))
    return pl.pallas_call(
        functools.partial(_fox_sample_kernel, pps=pps, layer=layer), grid_spec=grid_spec,
        out_shape=jax.ShapeDtypeStruct((b, N_HEADS, D_HEAD), F32),
        compiler_params=_cparams(("arbitrary", "arbitrary")), name="fox_sample",
    )(page_table, qb_t, st3, lf8, utri, ones, cache_k, cache_v, cache_f)


def _mla_sample_kernel(pt_ref, q_ref, kn_ref, c_hbm, r_hbm, o_ref, m_scr, l_scr, acc_scr,
                       cbuf, rbuf, csem, rsem, *, pps, layer):
    slot = _page_pipeline(pt_ref, [c_hbm, r_hbm], [cbuf, rbuf], [csem, rsem], layer, pps)
    c = pl.program_id(1)

    @pl.when(c == 0)
    def _():
        _init_state(m_scr, l_scr, acc_scr)

    q = q_ref[...]
    ckv = jnp.concatenate([cbuf[slot, p] for p in range(pps)], axis=0).astype(BF16)
    krt = jnp.concatenate([rbuf[slot, p] for p in range(pps)], axis=1).astype(BF16)
    s = _nt(q[:, 0:D_LATENT], ckv) + jnp.dot(q[:, D_LATENT:D_LATENT + D_ROPE], krt, preferred_element_type=F32)
    _dec_update(s, lambda p: jnp.dot(p, ckv, preferred_element_type=F32), m_scr, l_scr, acc_scr)

    @pl.when(c == pl.num_programs(1) - 1)
    def _():
        kn = kn_ref[...].astype(F32)
        s_self = jnp.sum(q.astype(F32) * kn, axis=1, keepdims=True)
        _dec_self(s_self, kn[:, 0:D_LATENT], m_scr, l_scr, acc_scr)
        o_ref[...] = acc_scr[...] / l_scr[...]


def _mla_sample(page_table, layer, qc_t, kvb3, cache_lat, cache_kr, pps):
    b, npages = page_table.shape
    grid_spec = pltpu.PrefetchScalarGridSpec(
        num_scalar_prefetch=1, grid=(b, npages // pps),
        in_specs=[_seq_spec(N_HEADS, 2 * LANES), pl.BlockSpec((None, 1, 2 * LANES), lambda b, c, pt: (b, 0, 2)),
                  _ANY, _ANY],
        out_specs=_seq_spec(N_HEADS, D_LATENT),
        scratch_shapes=[pltpu.VMEM((N_HEADS, LANES), F32), pltpu.VMEM((N_HEADS, LANES), F32),
                        pltpu.VMEM((N_HEADS, D_LATENT), F32)]
        + _page_scratch([(PAGE, D_LATENT), (D_ROPE, PAGE)], pps))
    return pl.pallas_call(
        functools.partial(_mla_sample_kernel, pps=pps, layer=layer), grid_spec=grid_spec,
        out_shape=jax.ShapeDtypeStruct((b, N_HEADS, D_LATENT), F32),
        compiler_params=_cparams(("arbitrary", "arbitrary")), name="mla_sample",
    )(page_table, qc_t, kvb3, cache_lat, cache_kr)


def _uv_kernel(o_ref, w_ref, out_ref):
    out = None
    for h in range(N_HEADS):
        d = jnp.dot(o_ref[h].astype(BF16), w_ref[h], preferred_element_type=F32)
        out = d if out is None else out + d
    out_ref[...] = out


def _uv_proj(olat_t, wuvpad):
    b = olat_t.shape[1]
    return pl.pallas_call(
        _uv_kernel, grid=(1,),
        in_specs=[pl.BlockSpec(olat_t.shape, lambda i: (0, 0, 0)), pl.BlockSpec(wuvpad.shape, lambda i: (0, 0, 0))],
        out_specs=pl.BlockSpec((b, BRANCH_W), lambda i: (0, 0)),
        out_shape=jax.ShapeDtypeStruct((b, BRANCH_W), F32),
        compiler_params=_cparams(("arbitrary",)), name="uv_proj",
    )(olat_t, wuvpad)


def _sample_layer(xs, layer, prep, tabs, caches, page_table, bias_s, topk, alpha, pps):
    ca_k, ca_v, ca_kidx, cb_k, cb_v, cb_f, cc_lat, cc_kr = caches
    bfar8, blast, bself8 = bias_s
    b = xs.shape[0]
    y = _project(xs, prep['wp'])
    qa, qi, qb, qc, st, kvb = _post(y, tabs, prep['post'])
    tr = lambda q: jnp.transpose(q, (1, 0, 2))
    g1 = st[:, 128:256]
    bc8 = lambda a: jnp.broadcast_to(a[:, :, None], (b, N_HEADS, LANES))
    w8 = bc8(g1[:, L_WIDX:L_WIDX + N_HEADS])
    lf8 = bc8(g1[:, L_BF:L_BF + N_HEADS])
    st3 = st[:, None, :]
    sc, selfsc = _idx_sample(page_table, layer, tr(qi), w8, g1[:, None, :], ca_kidx, pps)
    thr = _thresh(jnp.concatenate([sc[:, 0, :], selfsc[:, 0, :]], axis=1), topk)
    o_a = _dsa_sample(page_table, layer, tr(qa), sc, selfsc, thr[:, None, :], st3[:, :, 0:128], bfar8, blast, bself8,
                      ca_k, ca_v, pps)
    o_b = _fox_sample(page_table, layer, tr(qb), st3, lf8, cb_k, cb_v, cb_f, pps)
    olat = _mla_sample(page_table, layer, tr(qc), kvb[:, None, :], cc_lat, cc_kr, pps)
    o_c = _uv_proj(tr(olat), prep['wuvpad'])
    o_b = o_b.reshape(b, BRANCH_W)[:, jnp.asarray(_fox_head_perm())]
    xs = _merge(o_a.reshape(b, BRANCH_W), o_b, o_c, y, xs, prep['wbr'], prep['wout'], prep['lng'], prep['lnb'], alpha)
    return xs, st


def kernel(x_prompt, x_sample, cache_a_k, cache_a_v, cache_a_kidx, cache_b_k, cache_b_v, cache_b_logf,
           cache_c_latent, cache_c_krope, page_table, t5_table, w_in, b_forget, w_c_uq, g_c_q, g_c_kv,
           w_c_ukv, w_branch, w_out, ln_g, ln_b):
    depth = w_in.shape[0]
    bp, s, _ = x_prompt.shape
    bs, ds, _ = x_sample.shape
    npages = page_table.shape[1]
    past = npages * PAGE
    n_pool = cache_a_k.shape[1]
    assert bp == 1 and ds == 1 and cache_a_k.shape[2] == PAGE
    alpha = float((2 * depth) ** 0.25)
    pps = 64 if npages % 64 == 0 else npages // 2
    _, _, _, m1, m2 = _np_consts()
    m1, m2 = jnp.asarray(m1, BF16), jnp.asarray(m2, BF16)
    btile, bfar, qab = _bias_tables(t5_table)
    bc = lambda a: jnp.broadcast_to(a[:, None], (N_HEADS, LANES)).astype(F32)
    dist_last = PAGE - jnp.arange(PAGE)
    bias_s = (bc(bfar), jnp.transpose(t5_table[_t5_bucket(dist_last)]).astype(F32) * LOG2E,
              bc(t5_table[_t5_bucket(jnp.asarray(0))]) * LOG2E)
    tabs_p = _rope_tables(jnp.arange(s, dtype=I32))
    tabs_s = _rope_tables(jnp.full((bs,), past, I32))
    caches = (_pages_t(cache_a_k), _pages_t(cache_a_v), _pages_t(cache_a_kidx), _pages_t(cache_b_k),
              _pages_t(cache_b_v), _pages_t(cache_b_logf), cache_c_latent, _pages_t(cache_c_krope))
    xp = x_prompt.reshape(s, D_MODEL)
    xs = x_sample.reshape(bs, D_MODEL)
    rows_p, rows_s = [], []
    for l in range(depth):
        prep = _prep_layer(l, qab, w_in, b_forget, w_c_uq, g_c_q, g_c_kv, w_c_ukv, w_branch, w_out, ln_g, ln_b)
        xp, st_p = _prompt_layer(xp, prep, tabs_p, btile, min(TOPK_MAX, s // 4), alpha, m1, m2)
        rows_p.append(_state_rows(st_p, (bp, s)))
        xs, st_s = _sample_layer(xs, l, prep, tabs_s, caches, page_table, bias_s,
                                 min(TOPK_MAX, (past + 1) // 4), alpha, pps)
        rows_s.append(_state_rows(st_s, (bs, 1)))
    new_p = [jnp.stack([r[i] for r in rows_p], axis=0) for i in range(8)]
    new_s = [jnp.stack([r[i] for r in rows_s], axis=0) for i in range(8)]
    return (xp.reshape(bp, s, D_MODEL), xs.reshape(bs, 1, D_MODEL), *new_p, *new_s)
```
